```python
import math
import jax, jax.numpy as jnp
from jax import lax
import numpy as np

D_MODEL = 1024
BATCH = 4
SEQ = 8192
DEPTH = 1

HGRN_HEADS = 8
HGRN_DK = 128
HGRN_DV = D_MODEL // HGRN_HEADS
HGRN_KEY_WIDTH = HGRN_HEADS * HGRN_DK
HGRN_WIDTH = HGRN_HEADS * HGRN_DV
HGRN_CHUNK = 64

HYENA_WIDTH = D_MODEL
HYENA_SHORT = 3
HYENA_BANDS = 16
HYENA_EMB = 1 + 2 * HYENA_BANDS
HYENA_FILTER_HIDDEN = 64
HYENA_MIN_DECAY = -math.log(1e-2) / 1.5
HYENA_MAX_DECAY = -math.log(1e-2) / 0.3

PEER_HEADS = 8
PEER_NKEYS = 128
PEER_EXPERTS = PEER_NKEYS * PEER_NKEYS
PEER_QDIM = 256
PEER_TOPK = 16
PEER_TOKEN_BLOCK = 128

N_BRANCHES = 2
IN_SPLITS = (HGRN_KEY_WIDTH, 2 * HGRN_KEY_WIDTH, 3 * HGRN_KEY_WIDTH,
             3 * HGRN_KEY_WIDTH + HGRN_WIDTH, 3 * HGRN_KEY_WIDTH + 2 * HGRN_WIDTH,
             3 * HGRN_KEY_WIDTH + 2 * HGRN_WIDTH + 3 * HYENA_WIDTH)
IN_COLS = 3 * HGRN_KEY_WIDTH + 2 * HGRN_WIDTH + 3 * HYENA_WIDTH + N_BRANCHES * D_MODEL
RMS_EPS = 1e-6

kernel_name = "hgrn2_hyena_peer_hybrid_block"


def rmsnorm(x, g):
    xf = x.astype(jnp.float32)
    y = xf * lax.rsqrt(jnp.mean(xf * xf, axis=-1, keepdims=True) + RMS_EPS)
    return (y * g.astype(jnp.float32)).astype(x.dtype)


def hgrn2_chunk_scan(q, k, logf, v):
    n, s, h, dk = q.shape
    dv = v.shape[-1]
    nc = s // HGRN_CHUNK

    def to_chunks(a):
        return a.reshape(n, nc, HGRN_CHUNK, h, a.shape[-1]).transpose(1, 0, 3, 2, 4)

    qc, kc, fc, vc = to_chunks(q), to_chunks(k), to_chunks(logf), to_chunks(v)
    incl = jnp.tril(jnp.ones((HGRN_CHUNK, HGRN_CHUNK), dtype=bool))[:, :, None]

    def step(state, inp):
        qb, kb, fb, vb = inp
        b = jnp.cumsum(fb, axis=2)
        b_last = b[:, :, -1:, :]
        inter = jnp.einsum('nhtk,nhkv->nhtv', qb * jnp.exp(b), state)
        diff = b[:, :, :, None, :] - b[:, :, None, :, :]
        decay = jnp.exp(jnp.where(incl, diff, -jnp.inf))
        scores = jnp.einsum('nhtk,nhsk,nhtsk->nhts', qb, kb, decay)
        intra = jnp.einsum('nhts,nhsv->nhtv', scores, vb)
        new_state = (jnp.exp(b_last[:, :, 0, :, None]) * state
                     + jnp.einsum('nhsk,nhsv->nhkv', kb * jnp.exp(b_last - b), vb))
        return new_state, inter + intra

    s0 = jnp.zeros((n, h, dk, dv), jnp.float32)
    _, out = lax.scan(step, s0, (qc, kc, fc, vc))
    return out.transpose(1, 0, 3, 2, 4).reshape(n, s, h, dv)


def centred_short_conv(x, w, b):
    L = x.shape[1]
    pad = HYENA_SHORT // 2
    xp = jnp.pad(x, ((0, 0), (pad, pad), (0, 0)))
    y = b
    for j in range(HYENA_SHORT):
        y = y + xp[:, j:j + L] * w[j]
    return y


def hyena_filters(seq_len, w1, b1, freq1, w2, b2, freq2, w3, decay_rate):
    f32 = jnp.float32
    pos = jnp.arange(seq_len, dtype=f32)
    t = pos / max(seq_len - 1, 1)
    bands = jnp.linspace(1e-4, HYENA_BANDS - 1, HYENA_BANDS, dtype=f32)
    ang = (2.0 * math.pi / seq_len) * pos[:, None] * bands[None, :]
    z = jnp.concatenate([t[:, None], jnp.cos(ang), -jnp.sin(ang)], axis=-1)
    hid = jnp.sin(freq1.astype(f32) * (z @ w1.astype(f32) + b1.astype(f32)))
    hid = jnp.sin(freq2.astype(f32) * (hid @ w2.astype(f32) + b2.astype(f32)))
    filt = (hid @ w3.astype(f32)) * jnp.exp(-t[:, None] * jnp.abs(decay_rate.astype(f32))[None, :])
    return filt[:, :HYENA_WIDTH], filt[:, HYENA_WIDTH:]


def bidirectional_fftconv(u, h_fwd, h_bwd, bias):
    L = u.shape[1]
    kern = jnp.concatenate([h_fwd, jnp.zeros_like(h_fwd[:1]), h_bwd[1:][::-1]], axis=0)
    uf32 = u.astype(jnp.float32)
    uf = jnp.fft.rfft(uf32, n=2 * L, axis=1)
    kf = jnp.fft.rfft(kern, n=2 * L, axis=0)
    y = jnp.fft.irfft(uf * kf[None], n=2 * L, axis=1)[:, :L]
    return (y + uf32 * bias.astype(jnp.float32)).astype(u.dtype)


def hybrid_mixer(xn, w_in, lb, hgrn_norm_g, conv_w, conv_b, filt_w1, filt_b1, filt_freq1,
                 filt_w2, filt_b2, filt_freq2, filt_w3, filt_decay, hyena_bias,
                 w_branch_a, w_branch_b, w_out):
    B, L, _ = xn.shape
    f32 = jnp.float32
    proj = jnp.einsum('bld,dc->blc', xn, w_in)
    q, zf_fwd, zf_bwd, inp, og, hy, gates = jnp.split(proj, IN_SPLITS, axis=-1)

    def heads(a, d):
        return a.astype(f32).reshape(B, L, HGRN_HEADS, d)

    lbh = lb.reshape(HGRN_HEADS, HGRN_DK)

    def forget(z):
        zh = heads(z, HGRN_DK)
        logf = jnp.log(lbh + (1.0 - lbh) * jax.nn.sigmoid(zh))
        k = (1.0 - lbh) * jax.nn.sigmoid(-zh)
        return logf, k

    logf_f, k_f = forget(zf_fwd)
    logf_b, k_b = forget(zf_bwd)
    qh = jax.nn.silu(heads(q, HGRN_DK))
    vh = heads(inp, HGRN_DV)
    rev = lambda a: a[:, ::-1]
    o2 = hgrn2_chunk_scan(jnp.concatenate([qh, rev(qh)], axis=0),
                          jnp.concatenate([k_f, rev(k_b)], axis=0),
                          jnp.concatenate([logf_f, rev(logf_b)], axis=0),
                          jnp.concatenate([vh, rev(vh)], axis=0))
    o = o2[:B] + rev(o2[B:])
    o = rmsnorm(o, hgrn_norm_g).reshape(B, L, HGRN_WIDTH)
    y_a = jnp.einsum('blc,cd->bld', (o * jax.nn.silu(og.astype(f32))).astype(xn.dtype), w_branch_a)

    hy = centred_short_conv(hy, conv_w, conv_b)
    x0, x1, v = jnp.split(hy, 3, axis=-1)
    h_f, h_b = hyena_filters(L, filt_w1, filt_b1, filt_freq1, filt_w2, filt_b2, filt_freq2,
                             filt_w3, filt_decay)
    y_b = x0 * bidirectional_fftconv(v * x1, h_f, h_b, hyena_bias)
    y_b = jnp.einsum('blc,cd->bld', y_b, w_branch_b)

    g_a, g_b = jnp.split(gates, N_BRANCHES, axis=-1)
    merged = jax.nn.sigmoid(g_a) * y_a + jax.nn.sigmoid(g_b) * y_b
    return jnp.einsum('bld,de->ble', merged, w_out)


def peer_ffn(xn, w_q, subkeys, u_tab, v_tab):
    B, L, D = xn.shape
    blocks = xn.reshape(-1, PEER_TOKEN_BLOCK, D)
    sk = subkeys.astype(jnp.float32)

    def block(xb):
        T = xb.shape[0]
        qh = (xb @ w_q).astype(jnp.float32).reshape(T, PEER_HEADS, 2, PEER_QDIM // 2)
        s = jnp.einsum('thpc,hpnc->thpn', qh, sk)
        s1, i1 = lax.top_k(s[:, :, 0], PEER_TOPK)
        s2, i2 = lax.top_k(s[:, :, 1], PEER_TOPK)
        cand_s = (s1[..., :, None] + s2[..., None, :]).reshape(T, PEER_HEADS, PEER_TOPK * PEER_TOPK)
        cand_i = (i1[..., :, None] * PEER_NKEYS + i2[..., None, :]).reshape(T, PEER_HEADS, PEER_TOPK * PEER_TOPK)
        top_s, pos = lax.top_k(cand_s, PEER_TOPK)
        experts = jnp.take_along_axis(cand_i, pos, axis=-1)
        g = jax.nn.softmax(top_s, axis=-1).astype(xb.dtype)
        u = jnp.take(u_tab, experts, axis=0)
        v = jnp.take(v_tab, experts, axis=0)
        act = jax.nn.gelu(jnp.einsum('td,thkd->thk', xb, u)) * g
        return jnp.einsum('thk,thkd->td', act, v)

    return lax.map(block, blocks).reshape(B, L, D)


def setup_inputs(seed: int = 0) -> dict:
    key = jax.random.key(seed)
    ks = jax.random.split(key, 26)

    def nrm(k, shape, scale):
        return jax.random.normal(k, shape, jnp.float32) * scale

    decay_base = jnp.linspace(HYENA_MIN_DECAY, HYENA_MAX_DECAY, 2 * HYENA_WIDTH, dtype=jnp.float32)
    return {
        "x": nrm(ks[0], (BATCH, SEQ, D_MODEL), 1.0),
        "norm_mix_g": 1.0 + nrm(ks[1], (DEPTH, D_MODEL), 0.02),
        "w_in": nrm(ks[2], (DEPTH, D_MODEL, IN_COLS), D_MODEL ** -0.5),
        "hgrn_lb_logits": nrm(ks[3], (DEPTH + 1, HGRN_KEY_WIDTH), 0.1),
        "hgrn_norm_g": 1.0 + nrm(ks[4], (DEPTH, HGRN_DV), 0.02),
        "hyena_conv_w": nrm(ks[5], (DEPTH, HYENA_SHORT, 3 * HYENA_WIDTH), HYENA_SHORT ** -0.5),
        "hyena_conv_b": nrm(ks[6], (DEPTH, 3 * HYENA_WIDTH), 0.02),
        "filt_w1": nrm(ks[7], (DEPTH, HYENA_EMB, HYENA_FILTER_HIDDEN), HYENA_EMB ** -0.5),
        "filt_b1": nrm(ks[8], (DEPTH, HYENA_FILTER_HIDDEN), 0.1),
        "filt_freq1": 1.0 + nrm(ks[9], (DEPTH, HYENA_FILTER_HIDDEN), 0.02),
        "filt_w2": nrm(ks[10], (DEPTH, HYENA_FILTER_HIDDEN, HYENA_FILTER_HIDDEN), HYENA_FILTER_HIDDEN ** -0.5),
        "filt_b2": nrm(ks[11], (DEPTH, HYENA_FILTER_HIDDEN), 0.1),
        "filt_freq2": 1.0 + nrm(ks[12], (DEPTH, HYENA_FILTER_HIDDEN), 0.02),
        "filt_w3": nrm(ks[13], (DEPTH, HYENA_FILTER_HIDDEN, 2 * HYENA_WIDTH), 0.02 * HYENA_FILTER_HIDDEN ** -0.5),
        "filt_decay": decay_base[None, :] + nrm(ks[14], (DEPTH, 2 * HYENA_WIDTH), 0.1),
        "hyena_bias": nrm(ks[15], (DEPTH, HYENA_WIDTH), 0.1),
        "w_branch_a": nrm(ks[16], (DEPTH, HGRN_WIDTH, D_MODEL), HGRN_WIDTH ** -0.5),
        "w_branch_b": nrm(ks[17], (DEPTH, HYENA_WIDTH, D_MODEL), HYENA_WIDTH ** -0.5),
        "w_out": nrm(ks[18], (DEPTH, D_MODEL, D_MODEL), D_MODEL ** -0.5),
        "norm_ffn_g": 1.0 + nrm(ks[19], (DEPTH, D_MODEL), 0.02),
        "peer_w_q": nrm(ks[20], (DEPTH, D_MODEL, PEER_HEADS * PEER_QDIM), D_MODEL ** -0.5),
        "peer_subkeys": nrm(ks[21], (DEPTH, PEER_HEADS, 2, PEER_NKEYS, PEER_QDIM // 2), (PEER_QDIM // 2) ** -0.5),
        "peer_u": nrm(ks[22], (DEPTH, PEER_EXPERTS, D_MODEL), D_MODEL ** -0.5),
        "peer_v": nrm(ks[23], (DEPTH, PEER_EXPERTS, D_MODEL), 0.2),
        "norm_final_g": 1.0 + nrm(ks[24], (D_MODEL,), 0.02),
    }


def reference(x, norm_mix_g, w_in, hgrn_lb_logits, hgrn_norm_g, hyena_conv_w, hyena_conv_b,
              filt_w1, filt_b1, filt_freq1, filt_w2, filt_b2, filt_freq2, filt_w3, filt_decay,
              hyena_bias, w_branch_a, w_branch_b, w_out, norm_ffn_g, peer_w_q, peer_subkeys,
              peer_u, peer_v, norm_final_g):
    lb_table = jnp.cumsum(jax.nn.softmax(hgrn_lb_logits.astype(jnp.float32), axis=0), axis=0)
    h = x
    for layer in range(DEPTH):
        h = h + hybrid_mixer(rmsnorm(h, norm_mix_g[layer]), w_in[layer], lb_table[layer],
                             hgrn_norm_g[layer], hyena_conv_w[layer], hyena_conv_b[layer],
                             filt_w1[layer], filt_b1[layer], filt_freq1[layer],
                             filt_w2[layer], filt_b2[layer], filt_freq2[layer],
                             filt_w3[layer], filt_decay[layer], hyena_bias[layer],
                             w_branch_a[layer], w_branch_b[layer], w_out[layer])
        h = h + peer_ffn(rmsnorm(h, norm_ffn_g[layer]), peer_w_q[layer], peer_subkeys[layer],
                         peer_u[layer], peer_v[layer])
    return rmsnorm(h, norm_final_g)
```

```python
import functools
import math

import numpy as np
import jax
import jax.numpy as jnp
from jax import lax
from jax.experimental import pallas as pl
from jax.experimental.pallas import tpu as pltpu

F32 = jnp.float32
BF16 = jnp.bfloat16
I32 = jnp.int32

RMS_EPS = 1e-6
LANE = 128
SUBLANE = 8
VMEM_LIMIT = 56 * 1024 * 1024

HGRN_HEADS = 8
HGRN_DK = 128
HGRN_CHUNK = 128
HYENA_BANDS = 16
HYENA_HIDDEN = 64
PEER_HEADS = 8
PEER_NKEYS = 128
PEER_TOPK = 16

HIGHEST = lax.Precision.HIGHEST


def _cparams(sem, vmem=VMEM_LIMIT):
    return pltpu.CompilerParams(dimension_semantics=sem, vmem_limit_bytes=vmem)


def _dot(a, b):
    return jnp.dot(a, b, preferred_element_type=F32)


def _dot_nt(a, b):
    return lax.dot_general(a, b, (((1,), (1,)), ((), ())), preferred_element_type=F32)


def _normmm_kernel(x_ref, g_ref, w_ref, o_ref, *rest, emit_xn):
    if emit_xn:
        xn_out_ref, xn_ref = rest
    else:
        (xn_ref,) = rest

    @pl.when(pl.program_id(1) == 0)
    def _():
        x = x_ref[...]
        ms = jnp.mean(x * x, axis=-1, keepdims=True)
        xn = x * lax.rsqrt(ms + RMS_EPS) * g_ref[...]
        xn_ref[...] = xn.astype(BF16)
        if emit_xn:
            xn_out_ref[...] = xn

    o_ref[...] = _dot(xn_ref[...], w_ref[...])


def _norm_matmul(x, g, w_bf16, *, tm, tn, emit_xn=False):
    n, d = x.shape
    nout = w_bf16.shape[1]
    out_shape = [jax.ShapeDtypeStruct((n, nout), F32)]
    out_specs = [pl.BlockSpec((tm, tn), lambda i, j: (i, j))]
    if emit_xn:
        out_shape.append(jax.ShapeDtypeStruct((n, d), F32))
        out_specs.append(pl.BlockSpec((tm, d), lambda i, j: (i, 0)))
    res = pl.pallas_call(
        functools.partial(_normmm_kernel, emit_xn=emit_xn),
        grid=(n // tm, nout // tn),
        in_specs=[pl.BlockSpec((tm, d), lambda i, j: (i, 0)),
                  pl.BlockSpec((1, d), lambda i, j: (0, 0)),
                  pl.BlockSpec((d, tn), lambda i, j: (0, j))],
        out_specs=out_specs,
        out_shape=out_shape,
        scratch_shapes=[pltpu.VMEM((tm, d), BF16)],
        compiler_params=_cparams(("parallel", "arbitrary")),
        name="norm_matmul",
    )(x, g.reshape(1, d), w_bf16)
    return res if emit_xn else res[0]


def _hgrn_consts(c):
    nlev = int(math.log2(c))
    t = np.arange(c)
    m = np.zeros((2, (nlev + 2) * c, c), np.float32)
    up = np.zeros((2, nlev, c, LANE), np.float32)
    mask = np.zeros((nlev + 1, c, c), np.float32)
    for d in range(2):
        p = t if d == 0 else c - 1 - t
        m[d, 0:c] = p[None, :] <= p[:, None]
        for l in range(nlev):
            hs = 1 << l
            pmid = (p // (2 * hs)) * (2 * hs) + hs - 1
            m[d, (l + 1) * c:(l + 2) * c] = p[None, :] <= pmid[:, None]
            up[d, l] = (((p // hs) % 2) == 1)[:, None]
        m[d, (nlev + 1) * c:] = 1.0
    mask[0] = np.eye(c)
    for l in range(nlev):
        hs = 1 << l
        mask[l + 1] = (t[:, None] // (2 * hs)) == (t[None, :] // (2 * hs))
    return jnp.asarray(m, BF16), jnp.asarray(up, F32), jnp.asarray(mask, F32), nlev


def _split3(x):
    hi = x.astype(BF16)
    r1 = x - hi.astype(F32)
    mid = r1.astype(BF16)
    lo = (r1 - mid.astype(F32)).astype(BF16)
    return hi, mid, lo


def _hgrn_kernel(q_ref, z_ref, v_ref, og_ref, lb_ref, gn_ref, m_ref, up_ref, mask_ref,
                 o_ref, st_ref, of_ref, *, c, nc, nlev):
    d = pl.program_id(2)
    g = pl.program_id(3)
    ng = pl.num_programs(3)
    tg = nc * c

    @pl.when(g == 0)
    def _():
        st_ref[...] = jnp.zeros_like(st_ref)

    lb = lb_ref[...]
    grp = jnp.where(d == 0, g, ng - 1 - g)

    def chunk(i, carry):
        ci = jnp.where(d == 0, i, nc - 1 - i)
        r0 = pl.multiple_of(ci * c, c)
        qraw = q_ref[0, pl.ds(r0, c), :]
        z = z_ref[0, pl.ds(r0, c), :]
        v = v_ref[0, pl.ds(r0, c), :]
        logf = jnp.log(lb + (1.0 - lb) * jax.nn.sigmoid(z))
        kk = (1.0 - lb) * jax.nn.sigmoid(-z)
        q = qraw * jax.nn.sigmoid(qraw)

        l3 = jnp.concatenate(_split3(logf), axis=1)
        r = _dot(m_ref[0], l3)
        r = r[:, 0:LANE] + r[:, LANE:2 * LANE] + r[:, 2 * LANE:3 * LANE]
        b = r[0:c]
        blast = r[(nlev + 1) * c:(nlev + 1) * c + 1]

        qb = q.astype(BF16)
        kb = kk.astype(BF16)
        sc = mask_ref[0] * _dot_nt(qb, kb)
        for l in range(nlev):
            rl = r[(l + 1) * c:(l + 2) * c]
            upb = up_ref[0, l] > 0.5
            e = jnp.exp(jnp.minimum(jnp.where(upb, b - rl, rl - b), 0.0))
            qu = jnp.where(upb, q * e, 0.0).astype(BF16)
            kl = jnp.where(upb, 0.0, kk * e).astype(BF16)
            sc = sc + mask_ref[l + 1] * _dot_nt(qu, kl)

        vb = v.astype(BF16)
        st = st_ref[...]
        inter = _dot_nt((q * jnp.exp(b)).astype(BF16), st.astype(BF16))
        o = inter + _dot(sc.astype(BF16), vb)
        kd = (kk * jnp.exp(jnp.minimum(blast - b, 0.0))).astype(BF16)
        st_ref[...] = st * jnp.exp(blast) + _dot(v.T.astype(BF16), kd)

        row = pl.multiple_of(grp * tg + r0, c)

        @pl.when(d == 0)
        def _():
            of_ref[pl.ds(row, c), :] = o

        @pl.when(d == 1)
        def _():
            tot = of_ref[pl.ds(row, c), :] + o
            ms = jnp.mean(tot * tot, axis=-1, keepdims=True)
            on = tot * lax.rsqrt(ms + RMS_EPS) * gn_ref[...]
            og = og_ref[0, pl.ds(r0, c), :]
            o_ref[0, pl.ds(r0, c), :] = (on * (og * jax.nn.sigmoid(og))).astype(BF16)

        return carry

    lax.fori_loop(0, nc, chunk, 0)


def _hgrn(proj, lb, gn, *, tg):
    bsz, seq, _ = proj.shape
    c = HGRN_CHUNK
    nc = tg // c
    ng = seq // tg
    h = HGRN_HEADS
    m, up, mask, nlev = _hgrn_consts(c)

    def rows(d, g):
        return jnp.where(d == 0, g, ng - 1 - g)

    def outrows(d, g):
        return jnp.where(d == 0, ng - 1, ng - 1 - g)

    in_specs = [
        pl.BlockSpec((1, tg, LANE), lambda b, hh, d, g: (b, rows(d, g), hh)),
        pl.BlockSpec((1, tg, LANE), lambda b, hh, d, g: (b, rows(d, g), h * (1 + d) + hh)),
        pl.BlockSpec((1, tg, LANE), lambda b, hh, d, g: (b, rows(d, g), 3 * h + hh)),
        pl.BlockSpec((1, tg, LANE), lambda b, hh, d, g: (b, outrows(d, g), 4 * h + hh)),
        pl.BlockSpec((1, LANE), lambda b, hh, d, g: (0, hh)),
        pl.BlockSpec((1, LANE), lambda b, hh, d, g: (0, 0)),
        pl.BlockSpec((1,) + m.shape[1:], lambda b, hh, d, g: (d, 0, 0)),
        pl.BlockSpec((1,) + up.shape[1:], lambda b, hh, d, g: (d, 0, 0, 0)),
        pl.BlockSpec(mask.shape, lambda b, hh, d, g: (0, 0, 0)),
    ]
    return pl.pallas_call(
        functools.partial(_hgrn_kernel, c=c, nc=nc, nlev=nlev),
        grid=(bsz, h, 2, ng),
        in_specs=in_specs,
        out_specs=pl.BlockSpec((1, tg, LANE), lambda b, hh, d, g: (b, outrows(d, g), hh)),
        out_shape=jax.ShapeDtypeStruct((bsz, seq, h * LANE), BF16),
        scratch_shapes=[pltpu.VMEM((LANE, HGRN_DK), F32), pltpu.VMEM((seq, LANE), F32)],
        compiler_params=_cparams(("parallel", "parallel", "arbitrary", "arbitrary")),
        name="hgrn2_scan",
    )(proj, proj, proj, proj, lb.reshape(1, -1), gn.reshape(1, -1), m, up, mask)


def _shortconv_kernel(*refs, tc):
    (x0_ref, x0p_ref, x0n_ref, x1_ref, x1p_ref, x1n_ref, x2_ref, x2p_ref, x2n_ref,
     w0_ref, w1_ref, w2_ref, b0_ref, b1_ref, b2_ref, u_ref, x0c_ref) = refs
    i = pl.program_id(2)
    n = pl.num_programs(2)
    has_prev = (i > 0).astype(F32)
    has_next = (i < n - 1).astype(F32)
    rid = lax.broadcasted_iota(I32, (tc, LANE), 0)

    def conv(x_ref, xp_ref, xn_ref, w_ref, b_ref):
        x = x_ref[0]
        prev_row = xp_ref[0, SUBLANE - 1:SUBLANE, :] * has_prev
        next_row = xn_ref[0, 0:1, :] * has_next
        xm1 = jnp.where(rid == 0, prev_row, pltpu.roll(x, 1, 0))
        xp1 = jnp.where(rid == tc - 1, next_row, pltpu.roll(x, tc - 1, 0))
        w = w_ref[...]
        return ((b_ref[...] + xm1 * w[0:1]) + x * w[1:2]) + xp1 * w[2:3]

    x0c = conv(x0_ref, x0p_ref, x0n_ref, w0_ref, b0_ref)
    x1c = conv(x1_ref, x1p_ref, x1n_ref, w1_ref, b1_ref)
    vc = conv(x2_ref, x2p_ref, x2n_ref, w2_ref, b2_ref)
    x0c_ref[0] = x0c
    u_ref[0] = vc * x1c


def _shortconv(proj, conv_w, conv_b, *, col0, width, tc):
    bsz, seq, _ = proj.shape
    ncb = width // LANE
    cb0 = col0 // LANE
    nt = seq // tc
    rb = tc // SUBLANE
    nrb = seq // SUBLANE

    def stream(k):
        off = cb0 + k * ncb
        return [
            pl.BlockSpec((1, tc, LANE), lambda b, cc, i, off=off: (b, i, off + cc)),
            pl.BlockSpec((1, SUBLANE, LANE), lambda b, cc, i, off=off: (b, jnp.maximum(i * rb - 1, 0), off + cc)),
            pl.BlockSpec((1, SUBLANE, LANE), lambda b, cc, i, off=off: (b, jnp.minimum((i + 1) * rb, nrb - 1), off + cc)),
        ]

    wspecs = [pl.BlockSpec((3, LANE), lambda b, cc, i, k=k: (0, k * ncb + cc)) for k in range(3)]
    bspecs = [pl.BlockSpec((1, LANE), lambda b, cc, i, k=k: (0, k * ncb + cc)) for k in range(3)]
    ospec = pl.BlockSpec((1, tc, LANE), lambda b, cc, i: (b, i, cc))
    cb2 = conv_b.reshape(1, -1)
    return pl.pallas_call(
        functools.partial(_shortconv_kernel, tc=tc),
        grid=(bsz, ncb, nt),
        in_specs=stream(0) + stream(1) + stream(2) + wspecs + bspecs,
        out_specs=[ospec, ospec],
        out_shape=[jax.ShapeDtypeStruct((bsz, seq, width), F32)] * 2,
        compiler_params=_cparams(("parallel", "parallel", "arbitrary")),
        name="hyena_shortconv",
    )(*([proj] * 9), conv_w, conv_w, conv_w, cb2, cb2, cb2)


def _filter_kernel(band_ref, w1_ref, b1_ref, f1_ref, w2_ref, b2_ref, f2_ref, w3_ref, dec_ref, o_ref,
                   *, tl, seq):
    i = pl.program_id(0)
    pos = (i * tl + lax.broadcasted_iota(I32, (tl, LANE), 0)).astype(F32)
    lane = lax.broadcasted_iota(I32, (tl, LANE), 1)
    t = pos / float(max(seq - 1, 1))
    ang = (2.0 * math.pi / seq) * pos * band_ref[...]
    z = jnp.where(lane == 0, t,
                  jnp.where(lane <= HYENA_BANDS, jnp.cos(ang),
                            jnp.where(lane <= 2 * HYENA_BANDS, -jnp.sin(ang), 0.0)))
    hid = jnp.sin(f1_ref[...] * (jnp.dot(z, w1_ref[...], precision=HIGHEST, preferred_element_type=F32) + b1_ref[...]))
    hid = jnp.sin(f2_ref[...] * (jnp.dot(hid, w2_ref[...], precision=HIGHEST, preferred_element_type=F32) + b2_ref[...]))
    filt = jnp.dot(hid, w3_ref[...], precision=HIGHEST, preferred_element_type=F32)
    o_ref[...] = filt * jnp.exp(-t[:, 0:1] * jnp.abs(dec_ref[...]))


def _hyena_filters(seq, w1, b1, f1, w2, b2, f2, w3, decay, *, tl):
    emb, hid = w1.shape
    wout = w3.shape[1]
    bands = jnp.linspace(1e-4, HYENA_BANDS - 1, HYENA_BANDS, dtype=F32)
    band_row = jnp.zeros((1, LANE), F32).at[0, 1:1 + HYENA_BANDS].set(bands)
    band_row = band_row.at[0, 1 + HYENA_BANDS:1 + 2 * HYENA_BANDS].set(bands)

    def padm(a, r, c):
        return jnp.zeros((r, c), F32).at[:a.shape[0], :a.shape[1]].set(a.astype(F32))

    w1p = padm(w1, LANE, LANE)
    w2p = padm(w2, LANE, LANE)
    w3p = padm(w3, LANE, wout)
    vec = lambda a: padm(a.reshape(1, -1), 1, LANE)
    full = lambda shp: pl.BlockSpec(shp, lambda i: (0, 0))
    return pl.pallas_call(
        functools.partial(_filter_kernel, tl=tl, seq=seq),
        grid=(seq // tl,),
        in_specs=[full((1, LANE)), full((LANE, LANE)), full((1, LANE)), full((1, LANE)),
                  full((LANE, LANE)), full((1, LANE)), full((1, LANE)), full((LANE, wout)), full((1, wout))],
        out_specs=pl.BlockSpec((tl, wout), lambda i: (i, 0)),
        out_shape=jax.ShapeDtypeStruct((seq, wout), F32),
        compiler_params=_cparams(("parallel",)),
        name="hyena_filter_mlp",
    )(band_row, w1p, vec(b1), vec(f1), w2p, vec(b2), vec(f2), w3p, decay.reshape(1, -1).astype(F32))


def _fft_consts(n1):
    n2 = LANE
    n = n1 * n2
    k1 = np.arange(n1)
    f1 = np.exp(-2j * np.pi * np.outer(k1, k1) / n1)
    k2 = np.arange(n2)
    f2 = np.exp(-2j * np.pi * np.outer(k2, k2) / n2)
    tw = np.exp(-2j * np.pi * np.outer(k1, k2) / n)
    f1s = np.concatenate([f1.real, f1.imag], axis=0)
    g2 = np.block([[f2.real, f2.imag], [-f2.imag, f2.real]])
    g2c = np.block([[f2.real, -f2.imag], [f2.imag, f2.real]])
    twc = np.concatenate([tw.real, tw.imag], axis=1)

    def hilo(a):
        a32 = jnp.asarray(a, F32)
        hi = a32.astype(BF16)
        lo = (a32 - hi.astype(F32)).astype(BF16)
        return jnp.stack([hi, lo])

    return dict(f1s=hilo(f1s), g2=hilo(g2), g2c=hilo(g2c), tw=jnp.asarray(twc, F32))


def _mm_const_lhs(c_ref, x, passes):
    xh = x.astype(BF16)
    out = _dot(c_ref[0], xh)
    if passes >= 2:
        out = out + _dot(c_ref[0], (x - xh.astype(F32)).astype(BF16))
    if passes >= 3:
        out = out + _dot(c_ref[1], xh)
    return out


def _mm_const_rhs(x, c_ref, passes):
    xh = x.astype(BF16)
    out = _dot(xh, c_ref[0])
    if passes >= 2:
        out = out + _dot((x - xh.astype(F32)).astype(BF16), c_ref[0])
    if passes >= 3:
        out = out + _dot(xh, c_ref[1])
    return out


def _kfft_kernel(x_ref, f1s_ref, tw_ref, g2_ref, o_ref, *, n1, cb, scale, passes):
    tr = tw_ref[:, 0:LANE]
    ti = tw_ref[:, LANE:2 * LANE]

    def body(cidx, carry):
        x = x_ref[cidx]
        p = _mm_const_lhs(f1s_ref, x, passes)
        ar = p[0:n1]
        ai = p[n1:2 * n1]
        br = ar * tr - ai * ti
        bi = ar * ti + ai * tr
        o_ref[cidx] = _mm_const_rhs(jnp.concatenate([br, bi], axis=1), g2_ref, passes) * scale
        return carry

    lax.fori_loop(0, cb, body, 0)


def _kernel_spectrum(kern_t, consts, *, cb, passes):
    w, n1, _ = kern_t.shape
    full3 = lambda a: pl.BlockSpec(a.shape, lambda i: (0, 0, 0))
    return pl.pallas_call(
        functools.partial(_kfft_kernel, n1=n1, cb=cb, scale=1.0 / (n1 * LANE), passes=passes),
        grid=(w // cb,),
        in_specs=[pl.BlockSpec((cb, n1, LANE), lambda i: (i, 0, 0)),
                  full3(consts["f1s"]),
                  pl.BlockSpec(consts["tw"].shape, lambda i: (0, 0)),
                  full3(consts["g2"])],
        out_specs=pl.BlockSpec((cb, n1, 2 * LANE), lambda i: (i, 0, 0)),
        out_shape=jax.ShapeDtypeStruct((w, n1, 2 * LANE), F32),
        compiler_params=_cparams(("parallel",)),
        name="hyena_filter_fft",
    )(kern_t, consts["f1s"], consts["tw"], consts["g2"])


def _fftconv_kernel(u_ref, kf_ref, f1a_ref, f1b_ref, tw_ref, g2_ref, g2c_ref, o_ref, *, n1, cb, passes):
    n1h = n1 // 2
    tr = tw_ref[:, 0:LANE]
    ti = tw_ref[:, LANE:2 * LANE]

    def body(cidx, carry):
        z = jnp.concatenate([u_ref[0, cidx], u_ref[1, cidx]], axis=1)
        p = _mm_const_lhs(f1a_ref, z, passes)
        ar = p[0:n1, 0:LANE] - p[n1:, LANE:]
        ai = p[0:n1, LANE:] + p[n1:, 0:LANE]
        a2 = jnp.concatenate([ar * tr - ai * ti, ar * ti + ai * tr], axis=1)
        x = _mm_const_rhs(a2, g2_ref, passes)
        xr = x[:, 0:LANE]
        xi = x[:, LANE:]
        kf = kf_ref[cidx]
        kr = kf[:, 0:LANE]
        ki = kf[:, LANE:]
        y = jnp.concatenate([xr * kr - xi * ki, xr * ki + xi * kr], axis=1)
        bm = _mm_const_rhs(y, g2c_ref, passes)
        br = bm[:, 0:LANE]
        bi = bm[:, LANE:]
        b2 = jnp.concatenate([br * tr + bi * ti, bi * tr - br * ti], axis=1)
        p2 = _mm_const_lhs(f1b_ref, b2, passes)
        o_ref[0, cidx] = p2[0:n1h, 0:LANE] + p2[n1h:, LANE:]
        o_ref[1, cidx] = p2[0:n1h, LANE:] - p2[n1h:, 0:LANE]
        return carry

    lax.fori_loop(0, cb, body, 0)


def _fftconv(u_t, kf, consts, *, cb, passes):
    bsz, w, n1h, _ = u_t.shape
    n1 = 2 * n1h
    f1s = consts["f1s"]
    f1a = f1s[:, :, :n1h]
    f1b = jnp.concatenate([f1s[:, 0:n1h, :], f1s[:, n1:n1 + n1h, :]], axis=1)
    full3 = lambda a: pl.BlockSpec(a.shape, lambda i, j: (0, 0, 0))
    return pl.pallas_call(
        functools.partial(_fftconv_kernel, n1=n1, cb=cb, passes=passes),
        grid=(bsz // 2, w // cb),
        in_specs=[pl.BlockSpec((2, cb, n1h, LANE), lambda i, j: (i, j, 0, 0)),
                  pl.BlockSpec((cb, n1, 2 * LANE), lambda i, j: (j, 0, 0)),
                  full3(f1a), full3(f1b),
                  pl.BlockSpec(consts["tw"].shape, lambda i, j: (0, 0)),
                  full3(consts["g2"]), full3(consts["g2c"])],
        out_specs=pl.BlockSpec((2, cb, n1h, LANE), lambda i, j: (i, j, 0, 0)),
        out_shape=jax.ShapeDtypeStruct(u_t.shape, F32),
        compiler_params=_cparams(("parallel", "parallel")),
        name="hyena_fftconv",
    )(u_t, kf, f1a, f1b, consts["tw"], consts["g2"], consts["g2c"])


def _merge_kernel(oh_ref, yc_ref, u_ref, x0_ref, ga_ref, gb_ref, x_ref, hb_ref, wa_ref, wb_ref, wo_ref, o_ref):
    ya = _dot(oh_ref[...], wa_ref[...])
    ybp = x0_ref[...] * (yc_ref[...] + u_ref[...] * hb_ref[...])
    yb = _dot(ybp.astype(BF16), wb_ref[...])
    merged = jax.nn.sigmoid(ga_ref[...]) * ya + jax.nn.sigmoid(gb_ref[...]) * yb
    o_ref[...] = x_ref[...] + _dot(merged.astype(BF16), wo_ref[...])


def _merge(oh, yc, u, x0c, proj2, x2, hbias, wa, wb, wo, *, gate_col0, tm):
    n, d = x2.shape
    gc = gate_col0 // d
    row = lambda c=0: pl.BlockSpec((tm, d), lambda i, c=c: (i, c))
    wfull = pl.BlockSpec((d, d), lambda i: (0, 0))
    return pl.pallas_call(
        _merge_kernel,
        grid=(n // tm,),
        in_specs=[row(), row(), row(), row(), row(gc), row(gc + 1), row(),
                  pl.BlockSpec((1, d), lambda i: (0, 0)), wfull, wfull, wfull],
        out_specs=row(),
        out_shape=jax.ShapeDtypeStruct((n, d), F32),
        compiler_params=_cparams(("parallel",)),
        name="branch_merge",
    )(oh, yc, u, x0c, proj2, proj2, x2, hbias.reshape(1, d), wa, wb, wo)


def _extract_topk(s, k, vals_ref, idx_ref, ids=None):
    nrow = s.shape[0]
    rid = lax.broadcasted_iota(I32, s.shape, 0)
    for j in range(k):
        m = jnp.max(s, axis=0, keepdims=True)
        ix = jnp.min(jnp.where(s == m, rid, nrow), axis=0, keepdims=True)
        hit = rid == ix
        vals_ref[j:j + 1, :] = m
        if ids is None:
            idx_ref[j:j + 1, :] = ix
        else:
            idx_ref[j:j + 1, :] = jnp.max(jnp.where(hit, ids, -1), axis=0, keepdims=True)
        s = jnp.where(hit, -jnp.inf, s)


def _peer_topk_kernel(q_ref, sk_ref, e_ref, g_ref, s1_ref, i1_ref, s2_ref, i2_ref, ts_ref, te_ref):
    k = PEER_TOPK
    nk = PEER_NKEYS
    q = q_ref[...]
    s_a = _dot_nt(sk_ref[0, 0], q[:, 0:nk].astype(BF16))
    s_b = _dot_nt(sk_ref[0, 1], q[:, nk:2 * nk].astype(BF16))
    _extract_topk(s_a, k, s1_ref, i1_ref)
    _extract_topk(s_b, k, s2_ref, i2_ref)
    s1 = s1_ref[...]
    i1 = i1_ref[...]
    s2 = s2_ref[...]
    i2 = i2_ref[...]
    rid8 = lax.broadcasted_iota(I32, (SUBLANE, s1.shape[1]), 0)
    cs = [s1[0:1] + s2]
    ci = [i1[0:1] * nk + i2]
    for a in range(1, SUBLANE):
        nb = k // (a + 1)
        cs.append(jnp.where(rid8 < nb, s1[a:a + 1] + s2[0:SUBLANE], -jnp.inf))
        ci.append(i1[a:a + 1] * nk + i2[0:SUBLANE])
    cs.append(s1[SUBLANE:k] + s2[0:1])
    ci.append(i1[SUBLANE:k] * nk + i2[0:1])
    _extract_topk(jnp.concatenate(cs, axis=0), k, ts_ref, te_ref, ids=jnp.concatenate(ci, axis=0))
    ts = ts_ref[...]
    ex = jnp.exp(ts - ts[0:1])
    g_ref[0] = ex / jnp.sum(ex, axis=0, keepdims=True)
    e_ref[0] = te_ref[...]


def _peer_topk(qp, subkeys_bf16, *, tt):
    n = qp.shape[0]
    h = PEER_HEADS
    k = PEER_TOPK
    nk = PEER_NKEYS
    ospec = pl.BlockSpec((1, k, tt), lambda i, hh: (hh, 0, i))
    return pl.pallas_call(
        _peer_topk_kernel,
        grid=(n // tt, h),
        in_specs=[pl.BlockSpec((tt, 2 * nk), lambda i, hh: (i, hh)),
                  pl.BlockSpec((1, 2, nk, nk), lambda i, hh: (hh, 0, 0, 0))],
        out_specs=[ospec, ospec],
        out_shape=[jax.ShapeDtypeStruct((h, k, n), I32), jax.ShapeDtypeStruct((h, k, n), F32)],
        scratch_shapes=[pltpu.VMEM((k, tt), F32), pltpu.VMEM((k, tt), I32),
                        pltpu.VMEM((k, tt), F32), pltpu.VMEM((k, tt), I32),
                        pltpu.VMEM((k, tt), F32), pltpu.VMEM((k, tt), I32)],
        compiler_params=_cparams(("parallel", "arbitrary")),
        name="peer_topk",
    )(qp, subkeys_bf16)


def _expert_row(tab_ref, e):
    t = tab_ref[e >> 1].astype(F32)
    return jnp.where((e & 1) == 1, t[SUBLANE:], t[:SUBLANE])


def _sublane_sums8(ps, sub):
    lo4 = sub < 4
    c = []
    for i in range(4):
        a = jnp.where(lo4, ps[i], ps[i + 4])
        b = jnp.where(lo4, ps[i + 4], ps[i])
        c.append(a + pltpu.roll(b, 4, 0))
    m2 = (sub & 3) < 2
    dd = []
    for i in range(2):
        x = jnp.where(m2, c[i], pltpu.roll(c[i + 2], 2, 0))
        y = jnp.where(m2, pltpu.roll(c[i], 6, 0), c[i + 2])
        dd.append(x + y)
    m1 = (sub & 1) < 1
    x = jnp.where(m1, dd[0], pltpu.roll(dd[1], 1, 0))
    y = jnp.where(m1, pltpu.roll(dd[0], 7, 0), dd[1])
    return x + y


def _peer_u_kernel(idx_ref, x_ref, g_ref, tab_ref, a_ref, rall_ref, *, tb, npair):
    sub = lax.broadcasted_iota(I32, (SUBLANE, LANE), 0)
    ones = jnp.ones((SUBLANE, LANE), BF16)

    def token(t, carry):
        x = x_ref[t]

        def group(gi, c2):
            ps = [x * _expert_row(tab_ref, idx_ref[t, gi * SUBLANE + i]) for i in range(SUBLANE)]
            rall_ref[pl.ds(pl.multiple_of(gi * SUBLANE, SUBLANE), SUBLANE), :] = _sublane_sums8(ps, sub)
            return c2

        lax.fori_loop(0, npair // SUBLANE, group, 0)
        r = rall_ref[...]
        rh = r.astype(BF16)
        rl = (r - rh.astype(F32)).astype(BF16)
        s = (_dot_nt(ones, rh) + _dot_nt(ones, rl))[0:1]
        a_ref[pl.ds(t, 1), :] = jax.nn.gelu(s, approximate=True) * g_ref[pl.ds(t, 1), :]
        return carry

    lax.fori_loop(0, tb, token, 0)


def _peer_u(idx, x3, g, tab, *, tb):
    n, npair = idx.shape
    return pl.pallas_call(
        functools.partial(_peer_u_kernel, tb=tb, npair=npair),
        grid=(n // tb,),
        in_specs=[pl.BlockSpec((tb, npair), lambda i: (i, 0), memory_space=pltpu.SMEM),
                  pl.BlockSpec((tb, SUBLANE, LANE), lambda i: (i, 0, 0)),
                  pl.BlockSpec((tb, npair), lambda i: (i, 0)),
                  pl.BlockSpec(tab.shape, lambda i: (0, 0, 0), pipeline_mode=pl.Buffered(1))],
        out_specs=pl.BlockSpec((tb, npair), lambda i: (i, 0)),
        out_shape=jax.ShapeDtypeStruct((n, npair), F32),
        scratch_shapes=[pltpu.VMEM((npair, LANE), F32)],
        compiler_params=_cparams(("parallel",)),
        name="peer_expert_in",
    )(idx, x3, g, tab)


def _peer_v_kernel(idx_ref, a_ref, h_ref, gf_ref, tab_ref, o_ref, *, tb, npair):
    def token(t, carry):
        def group(gi, accs):
            accs = list(accs)
            for i in range(SUBLANE):
                j = gi * SUBLANE + i
                accs[i % 2] = accs[i % 2] + a_ref[t, j] * _expert_row(tab_ref, idx_ref[t, j])
            return tuple(accs)

        z = jnp.zeros((SUBLANE, LANE), F32)
        acc0, acc1 = lax.fori_loop(0, npair // SUBLANE, group, (z, z))
        hh = h_ref[t] + (acc0 + acc1)
        ss = jnp.sum(jnp.sum(hh * hh, axis=1, keepdims=True), axis=0, keepdims=True)
        o_ref[t] = hh * lax.rsqrt(ss * (1.0 / (SUBLANE * LANE)) + RMS_EPS) * gf_ref[...]
        return carry

    lax.fori_loop(0, tb, token, 0)


def _peer_v(idx, a, h3, gfin, tab, *, tb):
    n, npair = idx.shape
    smem = lambda: pl.BlockSpec((tb, npair), lambda i: (i, 0), memory_space=pltpu.SMEM)
    return pl.pallas_call(
        functools.partial(_peer_v_kernel, tb=tb, npair=npair),
        grid=(n // tb,),
        in_specs=[smem(), smem(),
                  pl.BlockSpec((tb, SUBLANE, LANE), lambda i: (i, 0, 0)),
                  pl.BlockSpec((SUBLANE, LANE), lambda i: (0, 0)),
                  pl.BlockSpec(tab.shape, lambda i: (0, 0, 0), pipeline_mode=pl.Buffered(1))],
        out_specs=pl.BlockSpec((tb, SUBLANE, LANE), lambda i: (i, 0, 0)),
        out_shape=jax.ShapeDtypeStruct(h3.shape, F32),
        compiler_params=_cparams(("parallel",)),
        name="peer_expert_out",
    )(idx, a, h3, gfin, tab)


def _pick(n, pref):
    t = min(n, pref)
    while n % t:
        t //= 2
    return t


def _layer(h, lb, p, fft_passes):
    bsz, seq, d = h.shape
    n = bsz * seq
    kw = HGRN_HEADS * HGRN_DK
    hyena_col0 = 3 * kw + 2 * d
    gate_col0 = hyena_col0 + 3 * d

    proj2 = _norm_matmul(h.reshape(n, d), p["norm_mix_g"], p["w_in"].astype(BF16),
                         tm=_pick(n, 1024), tn=1024)
    proj = proj2.reshape(bsz, seq, -1)

    oh = _hgrn(proj, lb, p["hgrn_norm_g"], tg=_pick(seq, 1024))

    u, x0c = _shortconv(proj, p["hyena_conv_w"], p["hyena_conv_b"], col0=hyena_col0, width=d,
                        tc=_pick(seq, 1024))
    filt = _hyena_filters(seq, p["filt_w1"], p["filt_b1"], p["filt_freq1"], p["filt_w2"], p["filt_b2"],
                          p["filt_freq2"], p["filt_w3"], p["filt_decay"], tl=_pick(seq, 512))
    h_f, h_b = filt[:, :d], filt[:, d:]
    kern = jnp.concatenate([h_f, jnp.zeros_like(h_f[:1]), h_b[1:][::-1]], axis=0)
    n1 = 2 * seq // LANE
    consts = _fft_consts(n1)
    kf = _kernel_spectrum(kern.T.reshape(d, n1, LANE), consts, cb=_pick(d, 8), passes=3)
    u_t = jnp.swapaxes(u, 1, 2).reshape(bsz, d, n1 // 2, LANE)
    y_t = _fftconv(u_t, kf, consts, cb=_pick(d, 16), passes=fft_passes)
    yc = jnp.swapaxes(y_t.reshape(bsz, d, seq), 1, 2)

    h1 = _merge(oh.reshape(n, d), yc.reshape(n, d), u.reshape(n, d), x0c.reshape(n, d), proj2,
                h.reshape(n, d), p["hyena_bias"], p["w_branch_a"].astype(BF16),
                p["w_branch_b"].astype(BF16), p["w_out"].astype(BF16), gate_col0=gate_col0, tm=_pick(n, 256))

    qp, hn = _norm_matmul(h1, p["norm_ffn_g"], p["peer_w_q"].astype(BF16), tm=_pick(n, 1024),
                          tn=_pick(p["peer_w_q"].shape[1], 1024), emit_xn=True)
    experts, gates = _peer_topk(qp, p["peer_subkeys"].astype(BF16), tt=_pick(n, 256))
    npair = PEER_HEADS * PEER_TOPK
    idx = experts.transpose(2, 0, 1).reshape(n, npair)
    gsm = gates.transpose(2, 0, 1).reshape(n, npair)
    ne = p["peer_u"].shape[0]
    utab = p["peer_u"].astype(BF16).reshape(ne // 2, 2 * SUBLANE, LANE)
    vtab = p["peer_v"].astype(BF16).reshape(ne // 2, 2 * SUBLANE, LANE)
    tb = _pick(n, 64)
    act = _peer_u(idx, hn.reshape(n, SUBLANE, LANE), gsm, utab, tb=tb)
    return h1, idx, act, vtab, tb


def kernel(x, norm_mix_g, w_in, hgrn_lb_logits, hgrn_norm_g, hyena_conv_w, hyena_conv_b, filt_w1, filt_b1, filt_freq1, filt_w2, filt_b2, filt_freq2, filt_w3, filt_decay, hyena_bias, w_branch_a, w_branch_b, w_out, norm_ffn_g, peer_w_q, peer_subkeys, peer_u, peer_v, norm_final_g):
    bsz, seq, d = x.shape
    n = bsz * seq
    depth = w_in.shape[0]
    assert depth == 1, "the fused final norm assumes a single layer"
    lb_table = jnp.cumsum(jax.nn.softmax(hgrn_lb_logits.astype(F32), axis=0), axis=0)
    p = dict(norm_mix_g=norm_mix_g[0], w_in=w_in[0], hgrn_norm_g=hgrn_norm_g[0], hyena_conv_w=hyena_conv_w[0],
             hyena_conv_b=hyena_conv_b[0], filt_w1=filt_w1[0], filt_b1=filt_b1[0], filt_freq1=filt_freq1[0],
             filt_w2=filt_w2[0], filt_b2=filt_b2[0], filt_freq2=filt_freq2[0], filt_w3=filt_w3[0],
             filt_decay=filt_decay[0], hyena_bias=hyena_bias[0], w_branch_a=w_branch_a[0],
             w_branch_b=w_branch_b[0], w_out=w_out[0], norm_ffn_g=norm_ffn_g[0], peer_w_q=peer_w_q[0],
             peer_subkeys=peer_subkeys[0], peer_u=peer_u[0], peer_v=peer_v[0])
    h1, idx, act, vtab, tb = _layer(x, lb_table[0], p, fft_passes=1)
    out = _peer_v(idx, act, h1.reshape(n, SUBLANE, LANE), norm_final_g.reshape(SUBLANE, LANE), vtab, tb=tb)
    return out.reshape(bsz, seq, d)
```

```python
import functools
import math

import numpy as np
import jax
import jax.numpy as jnp
from jax import lax
from jax.experimental import pallas as pl
from jax.experimental.pallas import tpu as pltpu

F32 = jnp.float32
BF16 = jnp.bfloat16
I32 = jnp.int32

RMS_EPS = 1e-6
LANE = 128
SUBLANE = 8
VMEM_LIMIT = 56 * 1024 * 1024

HGRN_HEADS = 8
HGRN_DK = 128
HGRN_CHUNK = 128
HYENA_BANDS = 16
HYENA_HIDDEN = 64
PEER_HEADS = 8
PEER_NKEYS = 128
PEER_TOPK = 16

HIGHEST = lax.Precision.HIGHEST


def _cparams(sem, vmem=VMEM_LIMIT):
    return pltpu.CompilerParams(dimension_semantics=sem, vmem_limit_bytes=vmem)


def _dot(a, b):
    return jnp.dot(a, b, preferred_element_type=F32)


def _dot_nt(a, b):
    return lax.dot_general(a, b, (((1,), (1,)), ((), ())), preferred_element_type=F32)


def _normmm_kernel(x_ref, g_ref, w_ref, o_ref, *rest, emit_xn):
    if emit_xn:
        xn_out_ref, xn_ref = rest
    else:
        (xn_ref,) = rest

    @pl.when(pl.program_id(1) == 0)
    def _():
        x = x_ref[...]
        ms = jnp.mean(x * x, axis=-1, keepdims=True)
        xn = x * lax.rsqrt(ms + RMS_EPS) * g_ref[...]
        xn_ref[...] = xn.astype(BF16)
        if emit_xn:
            xn_out_ref[...] = xn

    o_ref[...] = _dot(xn_ref[...], w_ref[...])


def _norm_matmul(x, g, w_bf16, *, tm, tn, emit_xn=False):
    n, d = x.shape
    nout = w_bf16.shape[1]
    out_shape = [jax.ShapeDtypeStruct((n, nout), F32)]
    out_specs = [pl.BlockSpec((tm, tn), lambda i, j: (i, j))]
    if emit_xn:
        out_shape.append(jax.ShapeDtypeStruct((n, d), F32))
        out_specs.append(pl.BlockSpec((tm, d), lambda i, j: (i, 0)))
    res = pl.pallas_call(
        functools.partial(_normmm_kernel, emit_xn=emit_xn),
        grid=(n // tm, nout // tn),
        in_specs=[pl.BlockSpec((tm, d), lambda i, j: (i, 0)),
                  pl.BlockSpec((1, d), lambda i, j: (0, 0)),
                  pl.BlockSpec((d, tn), lambda i, j: (0, j))],
        out_specs=out_specs,
        out_shape=out_shape,
        scratch_shapes=[pltpu.VMEM((tm, d), BF16)],
        compiler_params=_cparams(("parallel", "arbitrary")),
        name="norm_matmul",
    )(x, g.reshape(1, d), w_bf16)
    return res if emit_xn else res[0]


def _hgrn_consts(c):
    nlev = int(math.log2(c))
    t = np.arange(c)
    m = np.zeros((2, (nlev + 2) * c, c), np.float32)
    up = np.zeros((2, nlev, c, LANE), np.float32)
    mask = np.zeros((nlev + 1, c, c), np.float32)
    for d in range(2):
        p = t if d == 0 else c - 1 - t
        m[d, 0:c] = p[None, :] <= p[:, None]
        for l in range(nlev):
            hs = 1 << l
            pmid = (p // (2 * hs)) * (2 * hs) + hs - 1
            m[d, (l + 1) * c:(l + 2) * c] = p[None, :] <= pmid[:, None]
            up[d, l] = (((p // hs) % 2) == 1)[:, None]
        m[d, (nlev + 1) * c:] = 1.0
    mask[0] = np.eye(c)
    for l in range(nlev):
        hs = 1 << l
        mask[l + 1] = (t[:, None] // (2 * hs)) == (t[None, :] // (2 * hs))
    return jnp.asarray(m, BF16), jnp.asarray(up, F32), jnp.asarray(mask, F32), nlev


def _split3(x):
    hi = x.astype(BF16)
    r1 = x - hi.astype(F32)
    mid = r1.astype(BF16)
    lo = (r1 - mid.astype(F32)).astype(BF16)
    return hi, mid, lo


def _hgrn_kernel(q_ref, z_ref, v_ref, og_ref, lb_ref, gn_ref, m_ref, up_ref, mask_ref,
                 o_ref, st_ref, of_ref, *, c, nc, nlev, hpb):
    d = pl.program_id(2)
    g = pl.program_id(3)
    ng = pl.num_programs(3)
    tg = nc * c

    @pl.when(g == 0)
    def _():
        st_ref[...] = jnp.zeros_like(st_ref)

    grp = jnp.where(d == 0, g, ng - 1 - g)

    def head_chunk(hh, r0):
        sl = slice(hh * LANE, (hh + 1) * LANE)
        lb = lb_ref[:, sl]
        qraw = q_ref[0, pl.ds(r0, c), sl]
        z = z_ref[0, pl.ds(r0, c), sl]
        v = v_ref[0, pl.ds(r0, c), sl]
        logf = jnp.log(lb + (1.0 - lb) * jax.nn.sigmoid(z))
        kk = (1.0 - lb) * jax.nn.sigmoid(-z)
        q = qraw * jax.nn.sigmoid(qraw)

        l3 = jnp.concatenate(_split3(logf), axis=1)
        r = _dot(m_ref[0], l3)
        r = r[:, 0:LANE] + r[:, LANE:2 * LANE] + r[:, 2 * LANE:3 * LANE]
        b = r[0:c]
        blast = r[(nlev + 1) * c:(nlev + 1) * c + 1]

        qb = q.astype(BF16)
        kb = kk.astype(BF16)
        sc = mask_ref[0] * _dot_nt(qb, kb)
        for l in range(nlev):
            rl = r[(l + 1) * c:(l + 2) * c]
            upb = up_ref[0, l] > 0.5
            e = jnp.exp(jnp.minimum(jnp.where(upb, b - rl, rl - b), 0.0))
            qu = jnp.where(upb, q * e, 0.0).astype(BF16)
            kl = jnp.where(upb, 0.0, kk * e).astype(BF16)
            sc = sc + mask_ref[l + 1] * _dot_nt(qu, kl)

        vb = v.astype(BF16)
        st = st_ref[hh]
        inter = _dot_nt((q * jnp.exp(b)).astype(BF16), st.astype(BF16))
        o = inter + _dot(sc.astype(BF16), vb)
        kd = (kk * jnp.exp(jnp.minimum(blast - b, 0.0))).astype(BF16)
        st_ref[hh] = st * jnp.exp(blast) + _dot(v.T.astype(BF16), kd)
        return o

    def chunk(i, carry):
        ci = jnp.where(d == 0, i, nc - 1 - i)
        r0 = pl.multiple_of(ci * c, c)
        outs = [head_chunk(hh, r0) for hh in range(hpb)]
        row = pl.multiple_of(grp * tg + r0, c)

        @pl.when(d == 0)
        def _():
            for hh in range(hpb):
                of_ref[pl.ds(row, c), hh * LANE:(hh + 1) * LANE] = outs[hh]

        @pl.when(d == 1)
        def _():
            for hh in range(hpb):
                sl = slice(hh * LANE, (hh + 1) * LANE)
                tot = of_ref[pl.ds(row, c), sl] + outs[hh]
                ms = jnp.mean(tot * tot, axis=-1, keepdims=True)
                on = tot * lax.rsqrt(ms + RMS_EPS) * gn_ref[...]
                og = og_ref[0, pl.ds(r0, c), sl]
                o_ref[0, pl.ds(r0, c), sl] = (on * (og * jax.nn.sigmoid(og))).astype(BF16)

        return carry

    lax.fori_loop(0, nc, chunk, 0)


def _hgrn(proj, lb, gn, *, tg, hpb=2):
    bsz, seq, _ = proj.shape
    c = HGRN_CHUNK
    nc = tg // c
    ng = seq // tg
    h = HGRN_HEADS // hpb
    bw = hpb * LANE
    m, up, mask, nlev = _hgrn_consts(c)

    def rows(d, g):
        return jnp.where(d == 0, g, ng - 1 - g)

    def outrows(d, g):
        return jnp.where(d == 0, ng - 1, ng - 1 - g)

    in_specs = [
        pl.BlockSpec((1, tg, bw), lambda b, hh, d, g: (b, rows(d, g), hh)),
        pl.BlockSpec((1, tg, bw), lambda b, hh, d, g: (b, rows(d, g), h * (1 + d) + hh)),
        pl.BlockSpec((1, tg, bw), lambda b, hh, d, g: (b, rows(d, g), 3 * h + hh)),
        pl.BlockSpec((1, tg, bw), lambda b, hh, d, g: (b, outrows(d, g), 4 * h + hh)),
        pl.BlockSpec((1, bw), lambda b, hh, d, g: (0, hh)),
        pl.BlockSpec((1, LANE), lambda b, hh, d, g: (0, 0)),
        pl.BlockSpec((1,) + m.shape[1:], lambda b, hh, d, g: (d, 0, 0)),
        pl.BlockSpec((1,) + up.shape[1:], lambda b, hh, d, g: (d, 0, 0, 0)),
        pl.BlockSpec(mask.shape, lambda b, hh, d, g: (0, 0, 0)),
    ]
    return pl.pallas_call(
        functools.partial(_hgrn_kernel, c=c, nc=nc, nlev=nlev, hpb=hpb),
        grid=(bsz, h, 2, ng),
        in_specs=in_specs,
        out_specs=pl.BlockSpec((1, tg, bw), lambda b, hh, d, g: (b, outrows(d, g), hh)),
        out_shape=jax.ShapeDtypeStruct((bsz, seq, h * bw), BF16),
        scratch_shapes=[pltpu.VMEM((hpb, LANE, HGRN_DK), F32), pltpu.VMEM((seq, bw), F32)],
        compiler_params=_cparams(("parallel", "parallel", "arbitrary", "arbitrary")),
        name="hgrn2_scan",
    )(proj, proj, proj, proj, lb.reshape(1, -1), gn.reshape(1, -1), m, up, mask)


def _shortconv_kernel(*refs, tc):
    (x0_ref, x0p_ref, x0n_ref, x1_ref, x1p_ref, x1n_ref, x2_ref, x2p_ref, x2n_ref,
     w0_ref, w1_ref, w2_ref, b0_ref, b1_ref, b2_ref, u_ref, x0c_ref) = refs
    i = pl.program_id(2)
    n = pl.num_programs(2)
    has_prev = (i > 0).astype(F32)
    has_next = (i < n - 1).astype(F32)
    rid = lax.broadcasted_iota(I32, (tc, LANE), 0)

    def conv(x_ref, xp_ref, xn_ref, w_ref, b_ref):
        x = x_ref[0]
        prev_row = xp_ref[0, SUBLANE - 1:SUBLANE, :] * has_prev
        next_row = xn_ref[0, 0:1, :] * has_next
        xm1 = jnp.where(rid == 0, prev_row, pltpu.roll(x, 1, 0))
        xp1 = jnp.where(rid == tc - 1, next_row, pltpu.roll(x, tc - 1, 0))
        w = w_ref[...]
        return ((b_ref[...] + xm1 * w[0:1]) + x * w[1:2]) + xp1 * w[2:3]

    x0c = conv(x0_ref, x0p_ref, x0n_ref, w0_ref, b0_ref)
    x1c = conv(x1_ref, x1p_ref, x1n_ref, w1_ref, b1_ref)
    vc = conv(x2_ref, x2p_ref, x2n_ref, w2_ref, b2_ref)
    x0c_ref[0] = x0c
    u_ref[0] = vc * x1c


def _shortconv(proj, conv_w, conv_b, *, col0, width, tc):
    bsz, seq, _ = proj.shape
    ncb = width // LANE
    cb0 = col0 // LANE
    nt = seq // tc
    rb = tc // SUBLANE
    nrb = seq // SUBLANE

    def stream(k):
        off = cb0 + k * ncb
        return [
            pl.BlockSpec((1, tc, LANE), lambda b, cc, i, off=off: (b, i, off + cc)),
            pl.BlockSpec((1, SUBLANE, LANE), lambda b, cc, i, off=off: (b, jnp.maximum(i * rb - 1, 0), off + cc)),
            pl.BlockSpec((1, SUBLANE, LANE), lambda b, cc, i, off=off: (b, jnp.minimum((i + 1) * rb, nrb - 1), off + cc)),
        ]

    wspecs = [pl.BlockSpec((3, LANE), lambda b, cc, i, k=k: (0, k * ncb + cc)) for k in range(3)]
    bspecs = [pl.BlockSpec((1, LANE), lambda b, cc, i, k=k: (0, k * ncb + cc)) for k in range(3)]
    ospec = pl.BlockSpec((1, tc, LANE), lambda b, cc, i: (b, i, cc))
    cb2 = conv_b.reshape(1, -1)
    return pl.pallas_call(
        functools.partial(_shortconv_kernel, tc=tc),
        grid=(bsz, ncb, nt),
        in_specs=stream(0) + stream(1) + stream(2) + wspecs + bspecs,
        out_specs=[ospec, ospec],
        out_shape=[jax.ShapeDtypeStruct((bsz, seq, width), F32)] * 2,
        compiler_params=_cparams(("parallel", "parallel", "arbitrary")),
        name="hyena_shortconv",
    )(*([proj] * 9), conv_w, conv_w, conv_w, cb2, cb2, cb2)


def _filter_kernel(band_ref, w1_ref, b1_ref, f1_ref, w2_ref, b2_ref, f2_ref, w3_ref, dec_ref, o_ref,
                   *, tl, seq):
    i = pl.program_id(0)
    pos = (i * tl + lax.broadcasted_iota(I32, (tl, LANE), 0)).astype(F32)
    lane = lax.broadcasted_iota(I32, (tl, LANE), 1)
    t = pos / float(max(seq - 1, 1))
    ang = (2.0 * math.pi / seq) * pos * band_ref[...]
    z = jnp.where(lane == 0, t,
                  jnp.where(lane <= HYENA_BANDS, jnp.cos(ang),
                            jnp.where(lane <= 2 * HYENA_BANDS, -jnp.sin(ang), 0.0)))
    hid = jnp.sin(f1_ref[...] * (jnp.dot(z, w1_ref[...], precision=HIGHEST, preferred_element_type=F32) + b1_ref[...]))
    hid = jnp.sin(f2_ref[...] * (jnp.dot(hid, w2_ref[...], precision=HIGHEST, preferred_element_type=F32) + b2_ref[...]))
    filt = jnp.dot(hid, w3_ref[...], precision=HIGHEST, preferred_element_type=F32)
    o_ref[...] = filt * jnp.exp(-t[:, 0:1] * jnp.abs(dec_ref[...]))


def _hyena_filters(seq, w1, b1, f1, w2, b2, f2, w3, decay, *, tl):
    emb, hid = w1.shape
    wout = w3.shape[1]
    bands = jnp.linspace(1e-4, HYENA_BANDS - 1, HYENA_BANDS, dtype=F32)
    band_row = jnp.zeros((1, LANE), F32).at[0, 1:1 + HYENA_BANDS].set(bands)
    band_row = band_row.at[0, 1 + HYENA_BANDS:1 + 2 * HYENA_BANDS].set(bands)

    def padm(a, r, c):
        return jnp.zeros((r, c), F32).at[:a.shape[0], :a.shape[1]].set(a.astype(F32))

    w1p = padm(w1, LANE, LANE)
    w2p = padm(w2, LANE, LANE)
    w3p = padm(w3, LANE, wout)
    vec = lambda a: padm(a.reshape(1, -1), 1, LANE)
    full = lambda shp: pl.BlockSpec(shp, lambda i: (0, 0))
    return pl.pallas_call(
        functools.partial(_filter_kernel, tl=tl, seq=seq),
        grid=(seq // tl,),
        in_specs=[full((1, LANE)), full((LANE, LANE)), full((1, LANE)), full((1, LANE)),
                  full((LANE, LANE)), full((1, LANE)), full((1, LANE)), full((LANE, wout)), full((1, wout))],
        out_specs=pl.BlockSpec((tl, wout), lambda i: (i, 0)),
        out_shape=jax.ShapeDtypeStruct((seq, wout), F32),
        compiler_params=_cparams(("parallel",)),
        name="hyena_filter_mlp",
    )(band_row, w1p, vec(b1), vec(f1), w2p, vec(b2), vec(f2), w3p, decay.reshape(1, -1).astype(F32))


def _fft_consts(n1):
    n2 = LANE
    n = n1 * n2
    k1 = np.arange(n1)
    f1 = np.exp(-2j * np.pi * np.outer(k1, k1) / n1)
    k2 = np.arange(n2)
    f2 = np.exp(-2j * np.pi * np.outer(k2, k2) / n2)
    tw = np.exp(-2j * np.pi * np.outer(k1, k2) / n)
    f1s = np.concatenate([f1.real, f1.imag], axis=0)
    g2 = np.block([[f2.real, f2.imag], [-f2.imag, f2.real]])
    g2c = np.block([[f2.real, -f2.imag], [f2.imag, f2.real]])
    twc = np.concatenate([tw.real, tw.imag], axis=1)

    def hilo(a):
        a32 = jnp.asarray(a, F32)
        hi = a32.astype(BF16)
        lo = (a32 - hi.astype(F32)).astype(BF16)
        return jnp.stack([hi, lo])

    return dict(f1s=hilo(f1s), g2=hilo(g2), g2c=hilo(g2c), tw=jnp.asarray(twc, F32))


def _mm_const_lhs(c_ref, x, passes):
    xh = x.astype(BF16)
    out = _dot(c_ref[0], xh)
    if passes >= 2:
        out = out + _dot(c_ref[0], (x - xh.astype(F32)).astype(BF16))
    if passes >= 3:
        out = out + _dot(c_ref[1], xh)
    return out


def _mm_const_rhs(x, c_ref, passes):
    xh = x.astype(BF16)
    out = _dot(xh, c_ref[0])
    if passes >= 2:
        out = out + _dot((x - xh.astype(F32)).astype(BF16), c_ref[0])
    if passes >= 3:
        out = out + _dot(xh, c_ref[1])
    return out


def _kfft_kernel(x_ref, f1s_ref, tw_ref, g2_ref, o_ref, *, n1, cb, scale, passes):
    tr = tw_ref[:, 0:LANE]
    ti = tw_ref[:, LANE:2 * LANE]

    def body(cidx, carry):
        x = x_ref[cidx]
        p = _mm_const_lhs(f1s_ref, x, passes)
        ar = p[0:n1]
        ai = p[n1:2 * n1]
        br = ar * tr - ai * ti
        bi = ar * ti + ai * tr
        o_ref[cidx] = _mm_const_rhs(jnp.concatenate([br, bi], axis=1), g2_ref, passes) * scale
        return carry

    lax.fori_loop(0, cb, body, 0)


def _kernel_spectrum(kern_t, consts, *, cb, passes):
    w, n1, _ = kern_t.shape
    full3 = lambda a: pl.BlockSpec(a.shape, lambda i: (0, 0, 0))
    return pl.pallas_call(
        functools.partial(_kfft_kernel, n1=n1, cb=cb, scale=1.0 / (n1 * LANE), passes=passes),
        grid=(w // cb,),
        in_specs=[pl.BlockSpec((cb, n1, LANE), lambda i: (i, 0, 0)),
                  full3(consts["f1s"]),
                  pl.BlockSpec(consts["tw"].shape, lambda i: (0, 0)),
                  full3(consts["g2"])],
        out_specs=pl.BlockSpec((cb, n1, 2 * LANE), lambda i: (i, 0, 0)),
        out_shape=jax.ShapeDtypeStruct((w, n1, 2 * LANE), F32),
        compiler_params=_cparams(("parallel",)),
        name="hyena_filter_fft",
    )(kern_t, consts["f1s"], consts["tw"], consts["g2"])


def _fftconv_kernel(u_ref, kf_ref, f1a_ref, f1b_ref, tw_ref, g2_ref, g2c_ref, o_ref, *, n1, cb, passes):
    n1h = n1 // 2
    tr = tw_ref[:, 0:LANE]
    ti = tw_ref[:, LANE:2 * LANE]

    def body(cidx, carry):
        z = jnp.concatenate([u_ref[0, cidx], u_ref[1, cidx]], axis=1)
        p = _mm_const_lhs(f1a_ref, z, passes)
        ar = p[0:n1, 0:LANE] - p[n1:, LANE:]
        ai = p[0:n1, LANE:] + p[n1:, 0:LANE]
        a2 = jnp.concatenate([ar * tr - ai * ti, ar * ti + ai * tr], axis=1)
        x = _mm_const_rhs(a2, g2_ref, passes)
        xr = x[:, 0:LANE]
        xi = x[:, LANE:]
        kf = kf_ref[cidx]
        kr = kf[:, 0:LANE]
        ki = kf[:, LANE:]
        y = jnp.concatenate([xr * kr - xi * ki, xr * ki + xi * kr], axis=1)
        bm = _mm_const_rhs(y, g2c_ref, passes)
        br = bm[:, 0:LANE]
        bi = bm[:, LANE:]
        b2 = jnp.concatenate([br * tr + bi * ti, bi * tr - br * ti], axis=1)
        p2 = _mm_const_lhs(f1b_ref, b2, passes)
        o_ref[0, cidx] = p2[0:n1h, 0:LANE] + p2[n1h:, LANE:]
        o_ref[1, cidx] = p2[0:n1h, LANE:] - p2[n1h:, 0:LANE]
        return carry

    lax.fori_loop(0, cb, body, 0)


def _fftconv(u_t, kf, consts, *, cb, passes):
    bsz, w, n1h, _ = u_t.shape
    n1 = 2 * n1h
    f1s = consts["f1s"]
    f1a = f1s[:, :, :n1h]
    f1b = jnp.concatenate([f1s[:, 0:n1h, :], f1s[:, n1:n1 + n1h, :]], axis=1)
    full3 = lambda a: pl.BlockSpec(a.shape, lambda i, j: (0, 0, 0))
    return pl.pallas_call(
        functools.partial(_fftconv_kernel, n1=n1, cb=cb, passes=passes),
        grid=(bsz // 2, w // cb),
        in_specs=[pl.BlockSpec((2, cb, n1h, LANE), lambda i, j: (i, j, 0, 0)),
                  pl.BlockSpec((cb, n1, 2 * LANE), lambda i, j: (j, 0, 0)),
                  full3(f1a), full3(f1b),
                  pl.BlockSpec(consts["tw"].shape, lambda i, j: (0, 0)),
                  full3(consts["g2"]), full3(consts["g2c"])],
        out_specs=pl.BlockSpec((2, cb, n1h, LANE), lambda i, j: (i, j, 0, 0)),
        out_shape=jax.ShapeDtypeStruct(u_t.shape, F32),
        compiler_params=_cparams(("parallel", "parallel")),
        name="hyena_fftconv",
    )(u_t, kf, f1a, f1b, consts["tw"], consts["g2"], consts["g2c"])


def _merge_kernel(oh_ref, yc_ref, u_ref, x0_ref, ga_ref, gb_ref, x_ref, hb_ref, wa_ref, wb_ref, wo_ref, o_ref):
    ya = _dot(oh_ref[...], wa_ref[...])
    ybp = x0_ref[...] * (yc_ref[...] + u_ref[...] * hb_ref[...])
    yb = _dot(ybp.astype(BF16), wb_ref[...])
    merged = jax.nn.sigmoid(ga_ref[...]) * ya + jax.nn.sigmoid(gb_ref[...]) * yb
    o_ref[...] = x_ref[...] + _dot(merged.astype(BF16), wo_ref[...])


def _merge(oh, yc, u, x0c, proj2, x2, hbias, wa, wb, wo, *, gate_col0, tm):
    n, d = x2.shape
    gc = gate_col0 // d
    row = lambda c=0: pl.BlockSpec((tm, d), lambda i, c=c: (i, c))
    wfull = pl.BlockSpec((d, d), lambda i: (0, 0))
    return pl.pallas_call(
        _merge_kernel,
        grid=(n // tm,),
        in_specs=[row(), row(), row(), row(), row(gc), row(gc + 1), row(),
                  pl.BlockSpec((1, d), lambda i: (0, 0)), wfull, wfull, wfull],
        out_specs=row(),
        out_shape=jax.ShapeDtypeStruct((n, d), F32),
        compiler_params=_cparams(("parallel",)),
        name="branch_merge",
    )(oh, yc, u, x0c, proj2, proj2, x2, hbias.reshape(1, d), wa, wb, wo)


def _extract_topk(s, k, vals_ref, idx_ref, ids=None):
    nrow = s.shape[0]
    rid = lax.broadcasted_iota(I32, s.shape, 0)
    for j in range(k):
        m = jnp.max(s, axis=0, keepdims=True)
        ix = jnp.min(jnp.where(s == m, rid, nrow), axis=0, keepdims=True)
        hit = rid == ix
        vals_ref[j:j + 1, :] = m
        if ids is None:
            idx_ref[j:j + 1, :] = ix
        else:
            idx_ref[j:j + 1, :] = jnp.max(jnp.where(hit, ids, -1), axis=0, keepdims=True)
        s = jnp.where(hit, -jnp.inf, s)


def _peer_topk_kernel(q_ref, sk_ref, e_ref, g_ref, par_ref, s1_ref, i1_ref, s2_ref, i2_ref, ts_ref, te_ref):
    k = PEER_TOPK
    nk = PEER_NKEYS
    q = q_ref[...]
    s_a = _dot_nt(sk_ref[0, 0], q[:, 0:nk].astype(BF16))
    s_b = _dot_nt(sk_ref[0, 1], q[:, nk:2 * nk].astype(BF16))
    _extract_topk(s_a, k, s1_ref, i1_ref)
    _extract_topk(s_b, k, s2_ref, i2_ref)
    s1 = s1_ref[...]
    i1 = i1_ref[...]
    s2 = s2_ref[...]
    i2 = i2_ref[...]
    rid8 = lax.broadcasted_iota(I32, (SUBLANE, s1.shape[1]), 0)
    cs = [s1[0:1] + s2]
    ci = [i1[0:1] * nk + i2]
    for a in range(1, SUBLANE):
        nb = k // (a + 1)
        cs.append(jnp.where(rid8 < nb, s1[a:a + 1] + s2[0:SUBLANE], -jnp.inf))
        ci.append(i1[a:a + 1] * nk + i2[0:SUBLANE])
    cs.append(s1[SUBLANE:k] + s2[0:1])
    ci.append(i1[SUBLANE:k] * nk + i2[0:1])
    _extract_topk(jnp.concatenate(cs, axis=0), k, ts_ref, te_ref, ids=jnp.concatenate(ci, axis=0))
    ts = ts_ref[...]
    ex = jnp.exp(ts - ts[0:1])
    g_ref[0] = ex / jnp.sum(ex, axis=0, keepdims=True)
    te = te_ref[...]
    e_ref[0] = te >> 1
    par_ref[0] = (te & 1).astype(F32)


def _peer_topk(qp, subkeys_bf16, *, tt):
    n = qp.shape[0]
    h = PEER_HEADS
    k = PEER_TOPK
    nk = PEER_NKEYS
    ospec = pl.BlockSpec((1, k, tt), lambda i, hh: (hh, 0, i))
    return pl.pallas_call(
        _peer_topk_kernel,
        grid=(n // tt, h),
        in_specs=[pl.BlockSpec((tt, 2 * nk), lambda i, hh: (i, hh)),
                  pl.BlockSpec((1, 2, nk, nk), lambda i, hh: (hh, 0, 0, 0))],
        out_specs=[ospec, ospec, ospec],
        out_shape=[jax.ShapeDtypeStruct((h, k, n), I32), jax.ShapeDtypeStruct((h, k, n), F32),
                   jax.ShapeDtypeStruct((h, k, n), F32)],
        scratch_shapes=[pltpu.VMEM((k, tt), F32), pltpu.VMEM((k, tt), I32),
                        pltpu.VMEM((k, tt), F32), pltpu.VMEM((k, tt), I32),
                        pltpu.VMEM((k, tt), F32), pltpu.VMEM((k, tt), I32)],
        compiler_params=_cparams(("parallel", "arbitrary")),
        name="peer_topk",
    )(qp, subkeys_bf16)


def _lane_to_rows(row):
    return jnp.broadcast_to(row, (LANE, LANE)).T


def _sublane_sums8(ps, sub):
    lo4 = sub < 4
    c = []
    for i in range(4):
        a = jnp.where(lo4, ps[i], ps[i + 4])
        b = jnp.where(lo4, ps[i + 4], ps[i])
        c.append(a + pltpu.roll(b, 4, 0))
    m2 = (sub & 3) < 2
    dd = []
    for i in range(2):
        x = jnp.where(m2, c[i], pltpu.roll(c[i + 2], 2, 0))
        y = jnp.where(m2, pltpu.roll(c[i], 6, 0), c[i + 2])
        dd.append(x + y)
    m1 = (sub & 1) < 1
    x = jnp.where(m1, dd[0], pltpu.roll(dd[1], 1, 0))
    y = jnp.where(m1, pltpu.roll(dd[0], 7, 0), dd[1])
    return x + y


def _peer_u_kernel(idx_ref, x_ref, g_ref, par_ref, tab_ref, alo_ref, ahi_ref, rall_ref, pm_ref, *, tb, npair):
    sub = lax.broadcasted_iota(I32, (SUBLANE, LANE), 0)
    ones = jnp.ones((SUBLANE, LANE), BF16)

    def expand(t, carry):
        pm_ref[t] = _lane_to_rows(par_ref[pl.ds(t, 1), :])
        return carry

    def token(t, carry):
        x = x_ref[t]
        for gi in range(npair // SUBLANE):
            ps = []
            for i in range(SUBLANE):
                j = gi * SUBLANE + i
                tile = tab_ref[idx_ref[t, j]].astype(F32)
                odd = jnp.broadcast_to(pm_ref[t, j:j + 1, :], (SUBLANE, LANE)) > 0.5
                ps.append(x * jnp.where(odd, tile[SUBLANE:], tile[:SUBLANE]))
            rall_ref[t, gi * SUBLANE:(gi + 1) * SUBLANE, :] = _sublane_sums8(ps, sub)
        return carry

    def finish(t, carry):
        r = rall_ref[t]
        rh = r.astype(BF16)
        rl = (r - rh.astype(F32)).astype(BF16)
        s = (_dot_nt(ones, rh) + _dot_nt(ones, rl))[0:1]
        a = jax.nn.gelu(s, approximate=True) * g_ref[pl.ds(t, 1), :]
        ahi = a * par_ref[pl.ds(t, 1), :]
        ahi_ref[pl.ds(t, 1), :] = ahi
        alo_ref[pl.ds(t, 1), :] = a - ahi
        return carry

    lax.fori_loop(0, tb, expand, 0, unroll=4)
    lax.fori_loop(0, tb, token, 0, unroll=2)
    lax.fori_loop(0, tb, finish, 0, unroll=4)


def _peer_u(idx, x3, g, par, tab, *, tb):
    n, npair = idx.shape
    assert npair == LANE
    vrow = lambda: pl.BlockSpec((tb, npair), lambda i: (i, 0))
    return pl.pallas_call(
        functools.partial(_peer_u_kernel, tb=tb, npair=npair),
        grid=(n // tb,),
        in_specs=[pl.BlockSpec((tb, npair), lambda i: (i, 0), memory_space=pltpu.SMEM),
                  pl.BlockSpec((tb, SUBLANE, LANE), lambda i: (i, 0, 0)),
                  vrow(), vrow(),
                  pl.BlockSpec(tab.shape, lambda i: (0, 0, 0), pipeline_mode=pl.Buffered(1))],
        out_specs=[vrow(), vrow()],
        out_shape=[jax.ShapeDtypeStruct((n, npair), F32)] * 2,
        scratch_shapes=[pltpu.VMEM((tb, npair, LANE), F32), pltpu.VMEM((tb, npair, LANE), F32)],
        compiler_params=_cparams(("parallel",)),
        name="peer_expert_in",
    )(idx, x3, g, par, tab)


def _peer_v_kernel(idx_ref, alo_ref, ahi_ref, h_ref, gf_ref, tab_ref, o_ref, slo_ref, shi_ref, *, tb, npair):
    nacc = 4

    def expand(t, carry):
        slo_ref[t] = _lane_to_rows(alo_ref[pl.ds(t, 1), :])
        shi_ref[t] = _lane_to_rows(ahi_ref[pl.ds(t, 1), :])
        return carry

    def token(t, carry):
        accs = [None] * nacc
        for j in range(npair):
            tile = tab_ref[idx_ref[t, j]].astype(F32)
            term = slo_ref[t, j:j + 1, :] * tile[:SUBLANE] + shi_ref[t, j:j + 1, :] * tile[SUBLANE:]
            accs[j % nacc] = term if accs[j % nacc] is None else accs[j % nacc] + term
        hh = h_ref[t] + ((accs[0] + accs[1]) + (accs[2] + accs[3]))
        ss = jnp.sum(jnp.sum(hh * hh, axis=1, keepdims=True), axis=0, keepdims=True)
        o_ref[t] = hh * lax.rsqrt(ss * (1.0 / (SUBLANE * LANE)) + RMS_EPS) * gf_ref[...]
        return carry

    lax.fori_loop(0, tb, expand, 0, unroll=4)
    lax.fori_loop(0, tb, token, 0, unroll=2)


def _peer_v(idx, alo, ahi, h3, gfin, tab, *, tb):
    n, npair = idx.shape
    assert npair == LANE
    vrow = lambda: pl.BlockSpec((tb, npair), lambda i: (i, 0))
    return pl.pallas_call(
        functools.partial(_peer_v_kernel, tb=tb, npair=npair),
        grid=(n // tb,),
        in_specs=[pl.BlockSpec((tb, npair), lambda i: (i, 0), memory_space=pltpu.SMEM),
                  vrow(), vrow(),
                  pl.BlockSpec((tb, SUBLANE, LANE), lambda i: (i, 0, 0)),
                  pl.BlockSpec((SUBLANE, LANE), lambda i: (0, 0)),
                  pl.BlockSpec(tab.shape, lambda i: (0, 0, 0), pipeline_mode=pl.Buffered(1))],
        out_specs=pl.BlockSpec((tb, SUBLANE, LANE), lambda i: (i, 0, 0)),
        out_shape=jax.ShapeDtypeStruct(h3.shape, F32),
        scratch_shapes=[pltpu.VMEM((tb, npair, LANE), F32), pltpu.VMEM((tb, npair, LANE), F32)],
        compiler_params=_cparams(("parallel",)),
        name="peer_expert_out",
    )(idx, alo, ahi, h3, gfin, tab)


def _pick(n, pref):
    t = min(n, pref)
    while n % t:
        t //= 2
    return t


def _layer(h, lb, p, fft_passes):
    bsz, seq, d = h.shape
    n = bsz * seq
    kw = HGRN_HEADS * HGRN_DK
    hyena_col0 = 3 * kw + 2 * d
    gate_col0 = hyena_col0 + 3 * d

    proj2 = _norm_matmul(h.reshape(n, d), p["norm_mix_g"], p["w_in"].astype(BF16),
                         tm=_pick(n, 1024), tn=1024)
    proj = proj2.reshape(bsz, seq, -1)

    oh = _hgrn(proj, lb, p["hgrn_norm_g"], tg=_pick(seq, 1024))

    u, x0c = _shortconv(proj, p["hyena_conv_w"], p["hyena_conv_b"], col0=hyena_col0, width=d,
                        tc=_pick(seq, 1024))
    filt = _hyena_filters(seq, p["filt_w1"], p["filt_b1"], p["filt_freq1"], p["filt_w2"], p["filt_b2"],
                          p["filt_freq2"], p["filt_w3"], p["filt_decay"], tl=_pick(seq, 512))
    h_f, h_b = filt[:, :d], filt[:, d:]
    kern = jnp.concatenate([h_f, jnp.zeros_like(h_f[:1]), h_b[1:][::-1]], axis=0)
    n1 = 2 * seq // LANE
    consts = _fft_consts(n1)
    kf = _kernel_spectrum(kern.T.reshape(d, n1, LANE), consts, cb=_pick(d, 8), passes=3)
    u_t = jnp.swapaxes(u, 1, 2).reshape(bsz, d, n1 // 2, LANE)
    y_t = _fftconv(u_t, kf, consts, cb=_pick(d, 16), passes=fft_passes)
    yc = jnp.swapaxes(y_t.reshape(bsz, d, seq), 1, 2)

    h1 = _merge(oh.reshape(n, d), yc.reshape(n, d), u.reshape(n, d), x0c.reshape(n, d), proj2,
                h.reshape(n, d), p["hyena_bias"], p["w_branch_a"].astype(BF16),
                p["w_branch_b"].astype(BF16), p["w_out"].astype(BF16), gate_col0=gate_col0, tm=_pick(n, 256))

    qp, hn = _norm_matmul(h1, p["norm_ffn_g"], p["peer_w_q"].astype(BF16), tm=_pick(n, 1024),
                          tn=_pick(p["peer_w_q"].shape[1], 1024), emit_xn=True)
    tiles, gates, halves = _peer_topk(qp, p["peer_subkeys"].astype(BF16), tt=_pick(n, 256))
    npair = PEER_HEADS * PEER_TOPK
    tokmajor = lambda a: a.transpose(2, 0, 1).reshape(n, npair)
    idx = tokmajor(tiles)
    ne = p["peer_u"].shape[0]
    utab = p["peer_u"].astype(BF16).reshape(ne // 2, 2 * SUBLANE, LANE)
    vtab = p["peer_v"].astype(BF16).reshape(ne // 2, 2 * SUBLANE, LANE)
    tb = _pick(n, 64)
    alo, ahi = _peer_u(idx, hn.reshape(n, SUBLANE, LANE), tokmajor(gates), tokmajor(halves), utab, tb=tb)
    return h1, idx, alo, ahi, vtab, tb


def kernel(x, norm_mix_g, w_in, hgrn_lb_logits, hgrn_norm_g, hyena_conv_w, hyena_conv_b, filt_w1, filt_b1, filt_freq1, filt_w2, filt_b2, filt_freq2, filt_w3, filt_decay, hyena_bias, w_branch_a, w_branch_b, w_out, norm_ffn_g, peer_w_q, peer_subkeys, peer_u, peer_v, norm_final_g):
    bsz, seq, d = x.shape
    n = bsz * seq
    depth = w_in.shape[0]
    assert depth == 1, "the fused final norm assumes a single layer"
    lb_table = jnp.cumsum(jax.nn.softmax(hgrn_lb_logits.astype(F32), axis=0), axis=0)
    p = dict(norm_mix_g=norm_mix_g[0], w_in=w_in[0], hgrn_norm_g=hgrn_norm_g[0], hyena_conv_w=hyena_conv_w[0],
             hyena_conv_b=hyena_conv_b[0], filt_w1=filt_w1[0], filt_b1=filt_b1[0], filt_freq1=filt_freq1[0],
             filt_w2=filt_w2[0], filt_b2=filt_b2[0], filt_freq2=filt_freq2[0], filt_w3=filt_w3[0],
             filt_decay=filt_decay[0], hyena_bias=hyena_bias[0], w_branch_a=w_branch_a[0],
             w_branch_b=w_branch_b[0], w_out=w_out[0], norm_ffn_g=norm_ffn_g[0], peer_w_q=peer_w_q[0],
             peer_subkeys=peer_subkeys[0], peer_u=peer_u[0], peer_v=peer_v[0])
    h1, idx, alo, ahi, vtab, tb = _layer(x, lb_table[0], p, fft_passes=1)
    out = _peer_v(idx, alo, ahi, h1.reshape(n, SUBLANE, LANE), norm_final_g.reshape(SUBLANE, LANE), vtab, tb=tb)
    return out.reshape(bsz, seq, d)
```

```python
import functools
import math

import numpy as np
import jax
import jax.numpy as jnp
from jax import lax
from jax.experimental import pallas as pl
from jax.experimental.pallas import tpu as pltpu

F32 = jnp.float32
BF16 = jnp.bfloat16
I32 = jnp.int32

RMS_EPS = 1e-6
LANE = 128
SUBLANE = 8
VMEM_LIMIT = 56 * 1024 * 1024

HGRN_HEADS = 8
HGRN_DK = 128
HGRN_CHUNK = 128
HYENA_BANDS = 16
HYENA_HIDDEN = 64
PEER_HEADS = 8
PEER_NKEYS = 128
PEER_TOPK = 16

HIGHEST = lax.Precision.HIGHEST


def _cparams(sem, vmem=VMEM_LIMIT):
    return pltpu.CompilerParams(dimension_semantics=sem, vmem_limit_bytes=vmem)


def _dot(a, b):
    return jnp.dot(a, b, preferred_element_type=F32)


def _dot_nt(a, b):
    return lax.dot_general(a, b, (((1,), (1,)), ((), ())), preferred_element_type=F32)


def _normmm_kernel(x_ref, g_ref, w_ref, o_ref, *rest, emit_xn):
    if emit_xn:
        xn_out_ref, xn_ref = rest
    else:
        (xn_ref,) = rest

    @pl.when(pl.program_id(1) == 0)
    def _():
        x = x_ref[...]
        ms = jnp.mean(x * x, axis=-1, keepdims=True)
        xn = x * lax.rsqrt(ms + RMS_EPS) * g_ref[...]
        xn_ref[...] = xn.astype(BF16)
        if emit_xn:
            xn_out_ref[...] = xn

    o_ref[...] = _dot(xn_ref[...], w_ref[...])


def _norm_matmul(x, g, w_bf16, *, tm, tn, emit_xn=False):
    n, d = x.shape
    nout = w_bf16.shape[1]
    out_shape = [jax.ShapeDtypeStruct((n, nout), F32)]
    out_specs = [pl.BlockSpec((tm, tn), lambda i, j: (i, j))]
    if emit_xn:
        out_shape.append(jax.ShapeDtypeStruct((n, d), F32))
        out_specs.append(pl.BlockSpec((tm, d), lambda i, j: (i, 0)))
    res = pl.pallas_call(
        functools.partial(_normmm_kernel, emit_xn=emit_xn),
        grid=(n // tm, nout // tn),
        in_specs=[pl.BlockSpec((tm, d), lambda i, j: (i, 0)),
                  pl.BlockSpec((1, d), lambda i, j: (0, 0)),
                  pl.BlockSpec((d, tn), lambda i, j: (0, j))],
        out_specs=out_specs,
        out_shape=out_shape,
        scratch_shapes=[pltpu.VMEM((tm, d), BF16)],
        compiler_params=_cparams(("parallel", "arbitrary")),
        name="norm_matmul",
    )(x, g.reshape(1, d), w_bf16)
    return res if emit_xn else res[0]


def _hgrn_consts(c):
    nlev = int(math.log2(c))
    t = np.arange(c)
    m = np.zeros((2, (nlev + 2) * c, c), np.float32)
    up = np.zeros((2, nlev, c, LANE), np.float32)
    mask = np.zeros((nlev + 1, c, c), np.float32)
    for d in range(2):
        p = t if d == 0 else c - 1 - t
        m[d, 0:c] = p[None, :] <= p[:, None]
        for l in range(nlev):
            hs = 1 << l
            pmid = (p // (2 * hs)) * (2 * hs) + hs - 1
            m[d, (l + 1) * c:(l + 2) * c] = p[None, :] <= pmid[:, None]
            up[d, l] = (((p // hs) % 2) == 1)[:, None]
        m[d, (nlev + 1) * c:] = 1.0
    mask[0] = np.eye(c)
    for l in range(nlev):
        hs = 1 << l
        mask[l + 1] = (t[:, None] // (2 * hs)) == (t[None, :] // (2 * hs))
    return jnp.asarray(m, BF16), jnp.asarray(up, F32), jnp.asarray(mask, F32), nlev


def _split3(x):
    hi = x.astype(BF16)
    r1 = x - hi.astype(F32)
    mid = r1.astype(BF16)
    lo = (r1 - mid.astype(F32)).astype(BF16)
    return hi, mid, lo


def _hgrn_kernel(q_ref, z_ref, v_ref, og_ref, lb_ref, gn_ref, m_ref, up_ref, mask_ref,
                 o_ref, st_ref, of_ref, *, c, nc, nlev, hpb):
    d = pl.program_id(2)
    g = pl.program_id(3)
    ng = pl.num_programs(3)
    tg = nc * c

    @pl.when(g == 0)
    def _():
        st_ref[...] = jnp.zeros_like(st_ref)

    grp = jnp.where(d == 0, g, ng - 1 - g)

    def head_chunk(hh, r0):
        sl = slice(hh * LANE, (hh + 1) * LANE)
        lb = lb_ref[:, sl]
        qraw = q_ref[0, pl.ds(r0, c), sl]
        z = z_ref[0, pl.ds(r0, c), sl]
        v = v_ref[0, pl.ds(r0, c), sl]
        logf = jnp.log(lb + (1.0 - lb) * jax.nn.sigmoid(z))
        kk = (1.0 - lb) * jax.nn.sigmoid(-z)
        q = qraw * jax.nn.sigmoid(qraw)

        l3 = jnp.concatenate(_split3(logf), axis=1)
        r = _dot(m_ref[0], l3)
        r = r[:, 0:LANE] + r[:, LANE:2 * LANE] + r[:, 2 * LANE:3 * LANE]
        b = r[0:c]
        blast = r[(nlev + 1) * c:(nlev + 1) * c + 1]

        qb = q.astype(BF16)
        kb = kk.astype(BF16)
        sc = mask_ref[0] * _dot_nt(qb, kb)
        for l in range(nlev):
            rl = r[(l + 1) * c:(l + 2) * c]
            upb = up_ref[0, l] > 0.5
            e = jnp.exp(jnp.minimum(jnp.where(upb, b - rl, rl - b), 0.0))
            qu = jnp.where(upb, q * e, 0.0).astype(BF16)
            kl = jnp.where(upb, 0.0, kk * e).astype(BF16)
            sc = sc + mask_ref[l + 1] * _dot_nt(qu, kl)

        vb = v.astype(BF16)
        st = st_ref[hh]
        inter = _dot_nt((q * jnp.exp(b)).astype(BF16), st.astype(BF16))
        o = inter + _dot(sc.astype(BF16), vb)
        kd = (kk * jnp.exp(jnp.minimum(blast - b, 0.0))).astype(BF16)
        st_ref[hh] = st * jnp.exp(blast) + _dot(v.T.astype(BF16), kd)
        return o

    def chunk(i, carry):
        ci = jnp.where(d == 0, i, nc - 1 - i)
        r0 = pl.multiple_of(ci * c, c)
        outs = [head_chunk(hh, r0) for hh in range(hpb)]
        row = pl.multiple_of(grp * tg + r0, c)

        @pl.when(d == 0)
        def _():
            for hh in range(hpb):
                of_ref[pl.ds(row, c), hh * LANE:(hh + 1) * LANE] = outs[hh]

        @pl.when(d == 1)
        def _():
            for hh in range(hpb):
                sl = slice(hh * LANE, (hh + 1) * LANE)
                tot = of_ref[pl.ds(row, c), sl] + outs[hh]
                ms = jnp.mean(tot * tot, axis=-1, keepdims=True)
                on = tot * lax.rsqrt(ms + RMS_EPS) * gn_ref[...]
                og = og_ref[0, pl.ds(r0, c), sl]
                o_ref[0, pl.ds(r0, c), sl] = (on * (og * jax.nn.sigmoid(og))).astype(BF16)

        return carry

    lax.fori_loop(0, nc, chunk, 0)


def _hgrn(proj, lb, gn, *, tg, hpb=2):
    bsz, seq, _ = proj.shape
    c = HGRN_CHUNK
    nc = tg // c
    ng = seq // tg
    h = HGRN_HEADS // hpb
    bw = hpb * LANE
    m, up, mask, nlev = _hgrn_consts(c)

    def rows(d, g):
        return jnp.where(d == 0, g, ng - 1 - g)

    def outrows(d, g):
        return jnp.where(d == 0, ng - 1, ng - 1 - g)

    in_specs = [
        pl.BlockSpec((1, tg, bw), lambda b, hh, d, g: (b, rows(d, g), hh)),
        pl.BlockSpec((1, tg, bw), lambda b, hh, d, g: (b, rows(d, g), h * (1 + d) + hh)),
        pl.BlockSpec((1, tg, bw), lambda b, hh, d, g: (b, rows(d, g), 3 * h + hh)),
        pl.BlockSpec((1, tg, bw), lambda b, hh, d, g: (b, outrows(d, g), 4 * h + hh)),
        pl.BlockSpec((1, bw), lambda b, hh, d, g: (0, hh)),
        pl.BlockSpec((1, LANE), lambda b, hh, d, g: (0, 0)),
        pl.BlockSpec((1,) + m.shape[1:], lambda b, hh, d, g: (d, 0, 0)),
        pl.BlockSpec((1,) + up.shape[1:], lambda b, hh, d, g: (d, 0, 0, 0)),
        pl.BlockSpec(mask.shape, lambda b, hh, d, g: (0, 0, 0)),
    ]
    return pl.pallas_call(
        functools.partial(_hgrn_kernel, c=c, nc=nc, nlev=nlev, hpb=hpb),
        grid=(bsz, h, 2, ng),
        in_specs=in_specs,
        out_specs=pl.BlockSpec((1, tg, bw), lambda b, hh, d, g: (b, outrows(d, g), hh)),
        out_shape=jax.ShapeDtypeStruct((bsz, seq, h * bw), BF16),
        scratch_shapes=[pltpu.VMEM((hpb, LANE, HGRN_DK), F32), pltpu.VMEM((seq, bw), F32)],
        compiler_params=_cparams(("parallel", "parallel", "arbitrary", "arbitrary")),
        name="hgrn2_scan",
    )(proj, proj, proj, proj, lb.reshape(1, -1), gn.reshape(1, -1), m, up, mask)


def _shortconv_kernel(*refs, tc):
    (x0_ref, x0p_ref, x0n_ref, x1_ref, x1p_ref, x1n_ref, x2_ref, x2p_ref, x2n_ref,
     w0_ref, w1_ref, w2_ref, b0_ref, b1_ref, b2_ref, u_ref, x0c_ref) = refs
    i = pl.program_id(2)
    n = pl.num_programs(2)
    has_prev = (i > 0).astype(F32)
    has_next = (i < n - 1).astype(F32)
    rid = lax.broadcasted_iota(I32, (tc, LANE), 0)

    def conv(x_ref, xp_ref, xn_ref, w_ref, b_ref):
        x = x_ref[0]
        prev_row = xp_ref[0, SUBLANE - 1:SUBLANE, :] * has_prev
        next_row = xn_ref[0, 0:1, :] * has_next
        xm1 = jnp.where(rid == 0, prev_row, pltpu.roll(x, 1, 0))
        xp1 = jnp.where(rid == tc - 1, next_row, pltpu.roll(x, tc - 1, 0))
        w = w_ref[...]
        return ((b_ref[...] + xm1 * w[0:1]) + x * w[1:2]) + xp1 * w[2:3]

    x0c = conv(x0_ref, x0p_ref, x0n_ref, w0_ref, b0_ref)
    x1c = conv(x1_ref, x1p_ref, x1n_ref, w1_ref, b1_ref)
    vc = conv(x2_ref, x2p_ref, x2n_ref, w2_ref, b2_ref)
    x0c_ref[0] = x0c
    u_ref[0] = vc * x1c


def _shortconv(proj, conv_w, conv_b, *, col0, width, tc):
    bsz, seq, _ = proj.shape
    ncb = width // LANE
    cb0 = col0 // LANE
    nt = seq // tc
    rb = tc // SUBLANE
    nrb = seq // SUBLANE

    def stream(k):
        off = cb0 + k * ncb
        return [
            pl.BlockSpec((1, tc, LANE), lambda b, cc, i, off=off: (b, i, off + cc)),
            pl.BlockSpec((1, SUBLANE, LANE), lambda b, cc, i, off=off: (b, jnp.maximum(i * rb - 1, 0), off + cc)),
            pl.BlockSpec((1, SUBLANE, LANE), lambda b, cc, i, off=off: (b, jnp.minimum((i + 1) * rb, nrb - 1), off + cc)),
        ]

    wspecs = [pl.BlockSpec((3, LANE), lambda b, cc, i, k=k: (0, k * ncb + cc)) for k in range(3)]
    bspecs = [pl.BlockSpec((1, LANE), lambda b, cc, i, k=k: (0, k * ncb + cc)) for k in range(3)]
    ospec = pl.BlockSpec((1, tc, LANE), lambda b, cc, i: (b, i, cc))
    cb2 = conv_b.reshape(1, -1)
    return pl.pallas_call(
        functools.partial(_shortconv_kernel, tc=tc),
        grid=(bsz, ncb, nt),
        in_specs=stream(0) + stream(1) + stream(2) + wspecs + bspecs,
        out_specs=[ospec, ospec],
        out_shape=[jax.ShapeDtypeStruct((bsz, seq, width), F32)] * 2,
        compiler_params=_cparams(("parallel", "parallel", "arbitrary")),
        name="hyena_shortconv",
    )(*([proj] * 9), conv_w, conv_w, conv_w, cb2, cb2, cb2)


def _filter_kernel(band_ref, w1_ref, b1_ref, f1_ref, w2_ref, b2_ref, f2_ref, w3_ref, dec_ref, o_ref,
                   *, tl, seq):
    i = pl.program_id(0)
    pos = (i * tl + lax.broadcasted_iota(I32, (tl, LANE), 0)).astype(F32)
    lane = lax.broadcasted_iota(I32, (tl, LANE), 1)
    t = pos / float(max(seq - 1, 1))
    ang = (2.0 * math.pi / seq) * pos * band_ref[...]
    z = jnp.where(lane == 0, t,
                  jnp.where(lane <= HYENA_BANDS, jnp.cos(ang),
                            jnp.where(lane <= 2 * HYENA_BANDS, -jnp.sin(ang), 0.0)))
    hid = jnp.sin(f1_ref[...] * (jnp.dot(z, w1_ref[...], precision=HIGHEST, preferred_element_type=F32) + b1_ref[...]))
    hid = jnp.sin(f2_ref[...] * (jnp.dot(hid, w2_ref[...], precision=HIGHEST, preferred_element_type=F32) + b2_ref[...]))
    filt = jnp.dot(hid, w3_ref[...], precision=HIGHEST, preferred_element_type=F32)
    o_ref[...] = filt * jnp.exp(-t[:, 0:1] * jnp.abs(dec_ref[...]))


def _hyena_filters(seq, w1, b1, f1, w2, b2, f2, w3, decay, *, tl):
    emb, hid = w1.shape
    wout = w3.shape[1]
    bands = jnp.linspace(1e-4, HYENA_BANDS - 1, HYENA_BANDS, dtype=F32)
    band_row = jnp.zeros((1, LANE), F32).at[0, 1:1 + HYENA_BANDS].set(bands)
    band_row = band_row.at[0, 1 + HYENA_BANDS:1 + 2 * HYENA_BANDS].set(bands)

    def padm(a, r, c):
        return jnp.zeros((r, c), F32).at[:a.shape[0], :a.shape[1]].set(a.astype(F32))

    w1p = padm(w1, LANE, LANE)
    w2p = padm(w2, LANE, LANE)
    w3p = padm(w3, LANE, wout)
    vec = lambda a: padm(a.reshape(1, -1), 1, LANE)
    full = lambda shp: pl.BlockSpec(shp, lambda i: (0, 0))
    return pl.pallas_call(
        functools.partial(_filter_kernel, tl=tl, seq=seq),
        grid=(seq // tl,),
        in_specs=[full((1, LANE)), full((LANE, LANE)), full((1, LANE)), full((1, LANE)),
                  full((LANE, LANE)), full((1, LANE)), full((1, LANE)), full((LANE, wout)), full((1, wout))],
        out_specs=pl.BlockSpec((tl, wout), lambda i: (i, 0)),
        out_shape=jax.ShapeDtypeStruct((seq, wout), F32),
        compiler_params=_cparams(("parallel",)),
        name="hyena_filter_mlp",
    )(band_row, w1p, vec(b1), vec(f1), w2p, vec(b2), vec(f2), w3p, decay.reshape(1, -1).astype(F32))


def _fft_consts(n1):
    n2 = LANE
    n = n1 * n2
    k1 = np.arange(n1)
    f1 = np.exp(-2j * np.pi * np.outer(k1, k1) / n1)
    k2 = np.arange(n2)
    f2 = np.exp(-2j * np.pi * np.outer(k2, k2) / n2)
    tw = np.exp(-2j * np.pi * np.outer(k1, k2) / n)
    f1s = np.concatenate([f1.real, f1.imag], axis=0)
    g2 = np.block([[f2.real, f2.imag], [-f2.imag, f2.real]])
    g2c = np.block([[f2.real, -f2.imag], [f2.imag, f2.real]])
    twc = np.concatenate([tw.real, tw.imag], axis=1)

    def hilo(a):
        a32 = jnp.asarray(a, F32)
        hi = a32.astype(BF16)
        lo = (a32 - hi.astype(F32)).astype(BF16)
        return jnp.stack([hi, lo])

    return dict(f1s=hilo(f1s), g2=hilo(g2), g2c=hilo(g2c), tw=jnp.asarray(twc, F32))


def _mm_const_lhs(c_ref, x, passes):
    xh = x.astype(BF16)
    out = _dot(c_ref[0], xh)
    if passes >= 2:
        out = out + _dot(c_ref[0], (x - xh.astype(F32)).astype(BF16))
    if passes >= 3:
        out = out + _dot(c_ref[1], xh)
    return out


def _mm_const_rhs(x, c_ref, passes):
    xh = x.astype(BF16)
    out = _dot(xh, c_ref[0])
    if passes >= 2:
        out = out + _dot((x - xh.astype(F32)).astype(BF16), c_ref[0])
    if passes >= 3:
        out = out + _dot(xh, c_ref[1])
    return out


def _kfft_kernel(x_ref, f1s_ref, tw_ref, g2_ref, o_ref, *, n1, cb, scale, passes):
    tr = tw_ref[:, 0:LANE]
    ti = tw_ref[:, LANE:2 * LANE]

    def body(cidx, carry):
        x = x_ref[cidx]
        p = _mm_const_lhs(f1s_ref, x, passes)
        ar = p[0:n1]
        ai = p[n1:2 * n1]
        br = ar * tr - ai * ti
        bi = ar * ti + ai * tr
        o_ref[cidx] = _mm_const_rhs(jnp.concatenate([br, bi], axis=1), g2_ref, passes) * scale
        return carry

    lax.fori_loop(0, cb, body, 0, unroll=2)


def _kernel_spectrum(kern_t, consts, *, cb, passes):
    w, n1, _ = kern_t.shape
    full3 = lambda a: pl.BlockSpec(a.shape, lambda i: (0, 0, 0))
    return pl.pallas_call(
        functools.partial(_kfft_kernel, n1=n1, cb=cb, scale=1.0 / (n1 * LANE), passes=passes),
        grid=(w // cb,),
        in_specs=[pl.BlockSpec((cb, n1, LANE), lambda i: (i, 0, 0)),
                  full3(consts["f1s"]),
                  pl.BlockSpec(consts["tw"].shape, lambda i: (0, 0)),
                  full3(consts["g2"])],
        out_specs=pl.BlockSpec((cb, n1, 2 * LANE), lambda i: (i, 0, 0)),
        out_shape=jax.ShapeDtypeStruct((w, n1, 2 * LANE), F32),
        compiler_params=_cparams(("parallel",)),
        name="hyena_filter_fft",
    )(kern_t, consts["f1s"], consts["tw"], consts["g2"])


def _fftconv_kernel(u_ref, kf_ref, f1a_ref, f1b_ref, tw_ref, g2_ref, g2c_ref, o_ref, *, n1, cb, passes):
    n1h = n1 // 2
    tr = tw_ref[:, 0:LANE]
    ti = tw_ref[:, LANE:2 * LANE]

    def body(cidx, carry):
        z = jnp.concatenate([u_ref[0, cidx], u_ref[1, cidx]], axis=1)
        p = _mm_const_lhs(f1a_ref, z, passes)
        ar = p[0:n1, 0:LANE] - p[n1:, LANE:]
        ai = p[0:n1, LANE:] + p[n1:, 0:LANE]
        a2 = jnp.concatenate([ar * tr - ai * ti, ar * ti + ai * tr], axis=1)
        x = _mm_const_rhs(a2, g2_ref, passes)
        xr = x[:, 0:LANE]
        xi = x[:, LANE:]
        kf = kf_ref[cidx]
        kr = kf[:, 0:LANE]
        ki = kf[:, LANE:]
        y = jnp.concatenate([xr * kr - xi * ki, xr * ki + xi * kr], axis=1)
        bm = _mm_const_rhs(y, g2c_ref, passes)
        br = bm[:, 0:LANE]
        bi = bm[:, LANE:]
        b2 = jnp.concatenate([br * tr + bi * ti, bi * tr - br * ti], axis=1)
        p2 = _mm_const_lhs(f1b_ref, b2, passes)
        o_ref[0, cidx] = p2[0:n1h, 0:LANE] + p2[n1h:, LANE:]
        o_ref[1, cidx] = p2[0:n1h, LANE:] - p2[n1h:, 0:LANE]
        return carry

    lax.fori_loop(0, cb, body, 0, unroll=2)


def _fftconv(u_t, kf, consts, *, cb, passes):
    bsz, w, n1h, _ = u_t.shape
    n1 = 2 * n1h
    f1s = consts["f1s"]
    f1a = f1s[:, :, :n1h]
    f1b = jnp.concatenate([f1s[:, 0:n1h, :], f1s[:, n1:n1 + n1h, :]], axis=1)
    full3 = lambda a: pl.BlockSpec(a.shape, lambda i, j: (0, 0, 0))
    return pl.pallas_call(
        functools.partial(_fftconv_kernel, n1=n1, cb=cb, passes=passes),
        grid=(bsz // 2, w // cb),
        in_specs=[pl.BlockSpec((2, cb, n1h, LANE), lambda i, j: (i, j, 0, 0)),
                  pl.BlockSpec((cb, n1, 2 * LANE), lambda i, j: (j, 0, 0)),
                  full3(f1a), full3(f1b),
                  pl.BlockSpec(consts["tw"].shape, lambda i, j: (0, 0)),
                  full3(consts["g2"]), full3(consts["g2c"])],
        out_specs=pl.BlockSpec((2, cb, n1h, LANE), lambda i, j: (i, j, 0, 0)),
        out_shape=jax.ShapeDtypeStruct(u_t.shape, F32),
        compiler_params=_cparams(("parallel", "parallel")),
        name="hyena_fftconv",
    )(u_t, kf, f1a, f1b, consts["tw"], consts["g2"], consts["g2c"])


def _merge_kernel(oh_ref, yc_ref, u_ref, x0_ref, ga_ref, gb_ref, x_ref, hb_ref, wa_ref, wb_ref, wo_ref, o_ref):
    ya = _dot(oh_ref[...], wa_ref[...])
    ybp = x0_ref[...] * (yc_ref[...] + u_ref[...] * hb_ref[...])
    yb = _dot(ybp.astype(BF16), wb_ref[...])
    merged = jax.nn.sigmoid(ga_ref[...]) * ya + jax.nn.sigmoid(gb_ref[...]) * yb
    o_ref[...] = x_ref[...] + _dot(merged.astype(BF16), wo_ref[...])


def _merge(oh, yc, u, x0c, proj2, x2, hbias, wa, wb, wo, *, gate_col0, tm):
    n, d = x2.shape
    gc = gate_col0 // d
    row = lambda c=0: pl.BlockSpec((tm, d), lambda i, c=c: (i, c))
    wfull = pl.BlockSpec((d, d), lambda i: (0, 0))
    return pl.pallas_call(
        _merge_kernel,
        grid=(n // tm,),
        in_specs=[row(), row(), row(), row(), row(gc), row(gc + 1), row(),
                  pl.BlockSpec((1, d), lambda i: (0, 0)), wfull, wfull, wfull],
        out_specs=row(),
        out_shape=jax.ShapeDtypeStruct((n, d), F32),
        compiler_params=_cparams(("parallel",)),
        name="branch_merge",
    )(oh, yc, u, x0c, proj2, proj2, x2, hbias.reshape(1, d), wa, wb, wo)


def _extract_topk(s, k, vals_ref, idx_ref, ids=None):
    nrow = s.shape[0]
    rid = lax.broadcasted_iota(I32, s.shape, 0)
    for j in range(k):
        m = jnp.max(s, axis=0, keepdims=True)
        ix = jnp.min(jnp.where(s == m, rid, nrow), axis=0, keepdims=True)
        hit = rid == ix
        vals_ref[j:j + 1, :] = m
        if ids is None:
            idx_ref[j:j + 1, :] = ix
        else:
            idx_ref[j:j + 1, :] = jnp.max(jnp.where(hit, ids, -1), axis=0, keepdims=True)
        s = jnp.where(hit, -jnp.inf, s)


def _peer_topk_kernel(q_ref, sk_ref, e_ref, g_ref, par_ref, s1_ref, i1_ref, s2_ref, i2_ref, ts_ref, te_ref):
    k = PEER_TOPK
    nk = PEER_NKEYS
    q = q_ref[...]
    s_a = _dot_nt(sk_ref[0, 0], q[:, 0:nk].astype(BF16))
    s_b = _dot_nt(sk_ref[0, 1], q[:, nk:2 * nk].astype(BF16))
    _extract_topk(s_a, k, s1_ref, i1_ref)
    _extract_topk(s_b, k, s2_ref, i2_ref)
    s1 = s1_ref[...]
    i1 = i1_ref[...]
    s2 = s2_ref[...]
    i2 = i2_ref[...]
    rid8 = lax.broadcasted_iota(I32, (SUBLANE, s1.shape[1]), 0)
    cs = [s1[0:1] + s2]
    ci = [i1[0:1] * nk + i2]
    for a in range(1, SUBLANE):
        nb = k // (a + 1)
        cs.append(jnp.where(rid8 < nb, s1[a:a + 1] + s2[0:SUBLANE], -jnp.inf))
        ci.append(i1[a:a + 1] * nk + i2[0:SUBLANE])
    cs.append(s1[SUBLANE:k] + s2[0:1])
    ci.append(i1[SUBLANE:k] * nk + i2[0:1])
    _extract_topk(jnp.concatenate(cs, axis=0), k, ts_ref, te_ref, ids=jnp.concatenate(ci, axis=0))
    ts = ts_ref[...]
    ex = jnp.exp(ts - ts[0:1])
    g_ref[0] = ex / jnp.sum(ex, axis=0, keepdims=True)
    te = te_ref[...]
    e_ref[0] = te >> 1
    par_ref[0] = (te & 1).astype(F32)


def _peer_topk(qp, subkeys_bf16, *, tt):
    n = qp.shape[0]
    h = PEER_HEADS
    k = PEER_TOPK
    nk = PEER_NKEYS
    ospec = pl.BlockSpec((1, k, tt), lambda i, hh: (hh, 0, i))
    return pl.pallas_call(
        _peer_topk_kernel,
        grid=(n // tt, h),
        in_specs=[pl.BlockSpec((tt, 2 * nk), lambda i, hh: (i, hh)),
                  pl.BlockSpec((1, 2, nk, nk), lambda i, hh: (hh, 0, 0, 0))],
        out_specs=[ospec, ospec, ospec],
        out_shape=[jax.ShapeDtypeStruct((h, k, n), I32), jax.ShapeDtypeStruct((h, k, n), F32),
                   jax.ShapeDtypeStruct((h, k, n), F32)],
        scratch_shapes=[pltpu.VMEM((k, tt), F32), pltpu.VMEM((k, tt), I32),
                        pltpu.VMEM((k, tt), F32), pltpu.VMEM((k, tt), I32),
                        pltpu.VMEM((k, tt), F32), pltpu.VMEM((k, tt), I32)],
        compiler_params=_cparams(("parallel", "arbitrary")),
        name="peer_topk",
    )(qp, subkeys_bf16)


def _lane_to_rows(row):
    return jnp.broadcast_to(row, (LANE, LANE)).T


def _sublane_sums8(ps, sub):
    lo4 = sub < 4
    c = []
    for i in range(4):
        a = jnp.where(lo4, ps[i], ps[i + 4])
        b = jnp.where(lo4, ps[i + 4], ps[i])
        c.append(a + pltpu.roll(b, 4, 0))
    m2 = (sub & 3) < 2
    dd = []
    for i in range(2):
        x = jnp.where(m2, c[i], pltpu.roll(c[i + 2], 2, 0))
        y = jnp.where(m2, pltpu.roll(c[i], 6, 0), c[i + 2])
        dd.append(x + y)
    m1 = (sub & 1) < 1
    x = jnp.where(m1, dd[0], pltpu.roll(dd[1], 1, 0))
    y = jnp.where(m1, pltpu.roll(dd[0], 7, 0), dd[1])
    return x + y


def _peer_u_kernel(idx_ref, x_ref, g_ref, par_ref, tab_ref, alo_ref, ahi_ref, rall_ref, pm_ref, *, tb, npair):
    sub = lax.broadcasted_iota(I32, (SUBLANE, LANE), 0)

    def expand(t, carry):
        pm_ref[t] = _lane_to_rows(par_ref[pl.ds(t, 1), :])
        return carry

    def token(t, carry):
        x = x_ref[t]
        for gi in range(npair // SUBLANE):
            ps = []
            for i in range(SUBLANE):
                j = gi * SUBLANE + i
                tile = tab_ref[idx_ref[t, j]].astype(F32)
                odd = jnp.broadcast_to(pm_ref[t, j:j + 1, :], (SUBLANE, LANE)) > 0.5
                ps.append(x * jnp.where(odd, tile[SUBLANE:], tile[:SUBLANE]))
            rall_ref[t, gi * SUBLANE:(gi + 1) * SUBLANE, :] = _sublane_sums8(ps, sub)
        return carry

    def finish(t, carry):
        r = rall_ref[t]
        s = jnp.sum(r.T, axis=0, keepdims=True)
        a = jax.nn.gelu(s, approximate=True) * g_ref[pl.ds(t, 1), :]
        ahi = a * par_ref[pl.ds(t, 1), :]
        ahi_ref[pl.ds(t, 1), :] = ahi
        alo_ref[pl.ds(t, 1), :] = a - ahi
        return carry

    lax.fori_loop(0, tb, expand, 0, unroll=4)
    lax.fori_loop(0, tb, token, 0, unroll=2)
    lax.fori_loop(0, tb, finish, 0, unroll=4)


def _peer_u(idx, x3, g, par, tab, *, tb):
    n, npair = idx.shape
    assert npair == LANE
    vrow = lambda: pl.BlockSpec((tb, npair), lambda i: (i, 0))
    return pl.pallas_call(
        functools.partial(_peer_u_kernel, tb=tb, npair=npair),
        grid=(n // tb,),
        in_specs=[pl.BlockSpec((tb, npair), lambda i: (i, 0), memory_space=pltpu.SMEM),
                  pl.BlockSpec((tb, SUBLANE, LANE), lambda i: (i, 0, 0)),
                  vrow(), vrow(),
                  pl.BlockSpec(tab.shape, lambda i: (0, 0, 0), pipeline_mode=pl.Buffered(1))],
        out_specs=[vrow(), vrow()],
        out_shape=[jax.ShapeDtypeStruct((n, npair), F32)] * 2,
        scratch_shapes=[pltpu.VMEM((tb, npair, LANE), F32), pltpu.VMEM((tb, npair, LANE), F32)],
        compiler_params=_cparams(("parallel",)),
        name="peer_expert_in",
    )(idx, x3, g, par, tab)


def _peer_v_consts(npair):
    rows = 2 * SUBLANE
    col = np.arange(npair * rows)
    expand = (col[None, :] // rows) == np.arange(npair)[:, None]
    s = np.arange(SUBLANE)[:, None]
    plo = (col[None, :] % rows) == s
    phi = (col[None, :] % rows) == s + SUBLANE
    return jnp.asarray(expand, BF16), jnp.asarray(plo, F32), jnp.asarray(phi, F32)


def _peer_v_kernel(idx_ref, alo_ref, ahi_ref, h_ref, gf_ref, ex_ref, plo_ref, phi_ref, tab_ref, o_ref,
                   elo_ref, ehi_ref, *, tb, npair):
    half = npair // 2
    kh = half * 2 * SUBLANE

    def lane_expand(a):
        ah = a.astype(BF16)
        al = (a - ah.astype(F32)).astype(BF16)
        return _dot(ah, ex_ref[...]) + _dot(al, ex_ref[...])

    elo_ref[...] = lane_expand(alo_ref[...])
    ehi_ref[...] = lane_expand(ahi_ref[...])

    def token(t, carry):
        tiles = [tab_ref[idx_ref[t, j]] for j in range(npair)]
        rhs = jnp.concatenate([jnp.concatenate(tiles[:half], axis=0),
                               jnp.concatenate(tiles[half:], axis=0)], axis=1)
        coef = elo_ref[pl.ds(t, 1), :] * plo_ref[...] + ehi_ref[pl.ds(t, 1), :] * phi_ref[...]
        c2 = jnp.concatenate([coef[:, :kh], coef[:, kh:]], axis=0)
        ch = c2.astype(BF16)
        cl = (c2 - ch.astype(F32)).astype(BF16)
        r = _dot(jnp.concatenate([ch, cl], axis=0), rhs)
        acc = ((r[0:SUBLANE, 0:LANE] + r[SUBLANE:2 * SUBLANE, LANE:])
               + (r[2 * SUBLANE:3 * SUBLANE, 0:LANE] + r[3 * SUBLANE:, LANE:]))
        hh = h_ref[t] + acc
        ss = jnp.sum(jnp.sum(hh * hh, axis=1, keepdims=True), axis=0, keepdims=True)
        o_ref[t] = hh * lax.rsqrt(ss * (1.0 / (SUBLANE * LANE)) + RMS_EPS) * gf_ref[...]
        return carry

    lax.fori_loop(0, tb, token, 0, unroll=8)


def _peer_v(idx, alo, ahi, h3, gfin, tab, *, tb):
    n, npair = idx.shape
    assert npair == LANE
    ex, plo, phi = _peer_v_consts(npair)
    vrow = lambda: pl.BlockSpec((tb, npair), lambda i: (i, 0))
    full = lambda a: pl.BlockSpec(a.shape, lambda i: (0, 0))
    return pl.pallas_call(
        functools.partial(_peer_v_kernel, tb=tb, npair=npair),
        grid=(n // tb,),
        in_specs=[pl.BlockSpec((tb, npair), lambda i: (i, 0), memory_space=pltpu.SMEM),
                  vrow(), vrow(),
                  pl.BlockSpec((tb, SUBLANE, LANE), lambda i: (i, 0, 0)),
                  pl.BlockSpec((SUBLANE, LANE), lambda i: (0, 0)),
                  full(ex), full(plo), full(phi),
                  pl.BlockSpec(tab.shape, lambda i: (0, 0, 0), pipeline_mode=pl.Buffered(1))],
        out_specs=pl.BlockSpec((tb, SUBLANE, LANE), lambda i: (i, 0, 0)),
        out_shape=jax.ShapeDtypeStruct(h3.shape, F32),
        scratch_shapes=[pltpu.VMEM((tb, 2 * SUBLANE * npair), F32), pltpu.VMEM((tb, 2 * SUBLANE * npair), F32)],
        compiler_params=_cparams(("parallel",)),
        name="peer_expert_out",
    )(idx, alo, ahi, h3, gfin, ex, plo, phi, tab)


def _pick(n, pref):
    t = min(n, pref)
    while n % t:
        t //= 2
    return t


def _layer(h, lb, p, fft_passes):
    bsz, seq, d = h.shape
    n = bsz * seq
    kw = HGRN_HEADS * HGRN_DK
    hyena_col0 = 3 * kw + 2 * d
    gate_col0 = hyena_col0 + 3 * d

    proj2 = _norm_matmul(h.reshape(n, d), p["norm_mix_g"], p["w_in"].astype(BF16),
                         tm=_pick(n, 1024), tn=1024)
    proj = proj2.reshape(bsz, seq, -1)

    oh = _hgrn(proj, lb, p["hgrn_norm_g"], tg=_pick(seq, 1024))

    u, x0c = _shortconv(proj, p["hyena_conv_w"], p["hyena_conv_b"], col0=hyena_col0, width=d,
                        tc=_pick(seq, 1024))
    filt = _hyena_filters(seq, p["filt_w1"], p["filt_b1"], p["filt_freq1"], p["filt_w2"], p["filt_b2"],
                          p["filt_freq2"], p["filt_w3"], p["filt_decay"], tl=_pick(seq, 512))
    h_f, h_b = filt[:, :d], filt[:, d:]
    kern = jnp.concatenate([h_f, jnp.zeros_like(h_f[:1]), h_b[1:][::-1]], axis=0)
    n1 = 2 * seq // LANE
    consts = _fft_consts(n1)
    kf = _kernel_spectrum(kern.T.reshape(d, n1, LANE), consts, cb=_pick(d, 8), passes=3)
    u_t = jnp.swapaxes(u, 1, 2).reshape(bsz, d, n1 // 2, LANE)
    y_t = _fftconv(u_t, kf, consts, cb=_pick(d, 16), passes=fft_passes)
    yc = jnp.swapaxes(y_t.reshape(bsz, d, seq), 1, 2)

    h1 = _merge(oh.reshape(n, d), yc.reshape(n, d), u.reshape(n, d), x0c.reshape(n, d), proj2,
                h.reshape(n, d), p["hyena_bias"], p["w_branch_a"].astype(BF16),
                p["w_branch_b"].astype(BF16), p["w_out"].astype(BF16), gate_col0=gate_col0, tm=_pick(n, 256))

    qp, hn = _norm_matmul(h1, p["norm_ffn_g"], p["peer_w_q"].astype(BF16), tm=_pick(n, 1024),
                          tn=_pick(p["peer_w_q"].shape[1], 1024), emit_xn=True)
    tiles, gates, halves = _peer_topk(qp, p["peer_subkeys"].astype(BF16), tt=_pick(n, 256))
    npair = PEER_HEADS * PEER_TOPK
    tokmajor = lambda a: a.transpose(2, 0, 1).reshape(n, npair)
    idx = tokmajor(tiles)
    ne = p["peer_u"].shape[0]
    utab = p["peer_u"].astype(BF16).reshape(ne // 2, 2 * SUBLANE, LANE)
    vtab = p["peer_v"].astype(BF16).reshape(ne // 2, 2 * SUBLANE, LANE)
    tb = _pick(n, 64)
    alo, ahi = _peer_u(idx, hn.reshape(n, SUBLANE, LANE), tokmajor(gates), tokmajor(halves), utab, tb=tb)
    return h1, idx, alo, ahi, vtab, tb


def kernel(x, norm_mix_g, w_in, hgrn_lb_logits, hgrn_norm_g, hyena_conv_w, hyena_conv_b, filt_w1, filt_b1, filt_freq1, filt_w2, filt_b2, filt_freq2, filt_w3, filt_decay, hyena_bias, w_branch_a, w_branch_b, w_out, norm_ffn_g, peer_w_q, peer_subkeys, peer_u, peer_v, norm_final_g):
    bsz, seq, d = x.shape
    n = bsz * seq
    depth = w_in.shape[0]
    assert depth == 1, "the fused final norm assumes a single layer"
    lb_table = jnp.cumsum(jax.nn.softmax(hgrn_lb_logits.astype(F32), axis=0), axis=0)
    p = dict(norm_mix_g=norm_mix_g[0], w_in=w_in[0], hgrn_norm_g=hgrn_norm_g[0], hyena_conv_w=hyena_conv_w[0],
             hyena_conv_b=hyena_conv_b[0], filt_w1=filt_w1[0], filt_b1=filt_b1[0], filt_freq1=filt_freq1[0],
             filt_w2=filt_w2[0], filt_b2=filt_b2[0], filt_freq2=filt_freq2[0], filt_w3=filt_w3[0],
             filt_decay=filt_decay[0], hyena_bias=hyena_bias[0], w_branch_a=w_branch_a[0],
             w_branch_b=w_branch_b[0], w_out=w_out[0], norm_ffn_g=norm_ffn_g[0], peer_w_q=peer_w_q[0],
             peer_subkeys=peer_subkeys[0], peer_u=peer_u[0], peer_v=peer_v[0])
    h1, idx, alo, ahi, vtab, tb = _layer(x, lb_table[0], p, fft_passes=1)
    out = _peer_v(idx, alo, ahi, h1.reshape(n, SUBLANE, LANE), norm_final_g.reshape(SUBLANE, LANE), vtab, tb=tb)
    return out.reshape(bsz, seq, d)
```

```python
import functools
import math

import numpy as np
import jax
import jax.numpy as jnp
from jax import lax
from jax.experimental import pallas as pl
from jax.experimental.pallas import tpu as pltpu

F32 = jnp.float32
BF16 = jnp.bfloat16
I32 = jnp.int32

RMS_EPS = 1e-6
LANE = 128
SUBLANE = 8
VMEM_LIMIT = 56 * 1024 * 1024

HGRN_HEADS = 8
HGRN_DK = 128
HGRN_CHUNK = 128
HYENA_BANDS = 16
HYENA_HIDDEN = 64
PEER_HEADS = 8
PEER_NKEYS = 128
PEER_TOPK = 16

HIGHEST = lax.Precision.HIGHEST


def _cparams(sem, vmem=VMEM_LIMIT):
    return pltpu.CompilerParams(dimension_semantics=sem, vmem_limit_bytes=vmem)


def _dot(a, b):
    return jnp.dot(a, b, preferred_element_type=F32)


def _dot_nt(a, b):
    return lax.dot_general(a, b, (((1,), (1,)), ((), ())), preferred_element_type=F32)


def _normmm_kernel(x_ref, g_ref, w_ref, o_ref, *rest, emit_xn):
    if emit_xn:
        xn_out_ref, xn_ref = rest
    else:
        (xn_ref,) = rest

    @pl.when(pl.program_id(1) == 0)
    def _():
        x = x_ref[...]
        ms = jnp.mean(x * x, axis=-1, keepdims=True)
        xn = x * lax.rsqrt(ms + RMS_EPS) * g_ref[...]
        xn_ref[...] = xn.astype(BF16)
        if emit_xn:
            xn_out_ref[...] = xn

    o_ref[...] = _dot(xn_ref[...], w_ref[...])


def _norm_matmul(x, g, w_bf16, *, tm, tn, emit_xn=False):
    n, d = x.shape
    nout = w_bf16.shape[1]
    out_shape = [jax.ShapeDtypeStruct((n, nout), F32)]
    out_specs = [pl.BlockSpec((tm, tn), lambda i, j: (i, j))]
    if emit_xn:
        out_shape.append(jax.ShapeDtypeStruct((n, d), F32))
        out_specs.append(pl.BlockSpec((tm, d), lambda i, j: (i, 0)))
    res = pl.pallas_call(
        functools.partial(_normmm_kernel, emit_xn=emit_xn),
        grid=(n // tm, nout // tn),
        in_specs=[pl.BlockSpec((tm, d), lambda i, j: (i, 0)),
                  pl.BlockSpec((1, d), lambda i, j: (0, 0)),
                  pl.BlockSpec((d, tn), lambda i, j: (0, j))],
        out_specs=out_specs,
        out_shape=out_shape,
        scratch_shapes=[pltpu.VMEM((tm, d), BF16)],
        compiler_params=_cparams(("parallel", "arbitrary")),
        name="norm_matmul",
    )(x, g.reshape(1, d), w_bf16)
    return res if emit_xn else res[0]


def _hgrn_consts(c):
    nlev = int(math.log2(c))
    t = np.arange(c)
    m = np.zeros((2, (nlev + 2) * c, c), np.float32)
    up = np.zeros((2, nlev, c, LANE), np.float32)
    mask = np.zeros((nlev + 1, c, c), np.float32)
    for d in range(2):
        p = t if d == 0 else c - 1 - t
        m[d, 0:c] = p[None, :] <= p[:, None]
        for l in range(nlev):
            hs = 1 << l
            pmid = (p // (2 * hs)) * (2 * hs) + hs - 1
            m[d, (l + 1) * c:(l + 2) * c] = p[None, :] <= pmid[:, None]
            up[d, l] = (((p // hs) % 2) == 1)[:, None]
        m[d, (nlev + 1) * c:] = 1.0
    mask[0] = np.eye(c)
    for l in range(nlev):
        hs = 1 << l
        mask[l + 1] = (t[:, None] // (2 * hs)) == (t[None, :] // (2 * hs))
    m3 = np.concatenate([m, m, m], axis=2)
    return jnp.asarray(m3, BF16), jnp.asarray(up, F32), jnp.asarray(mask, F32), nlev


def _split3(x):
    hi = x.astype(BF16)
    r1 = x - hi.astype(F32)
    mid = r1.astype(BF16)
    lo = (r1 - mid.astype(F32)).astype(BF16)
    return hi, mid, lo


def _hgrn_kernel(q_ref, z_ref, v_ref, og_ref, lb_ref, gn_ref, m_ref, up_ref, mask_ref,
                 o_ref, st_ref, of_ref, *, c, nc, nlev, hpb):
    d = pl.program_id(2)
    g = pl.program_id(3)
    ng = pl.num_programs(3)
    tg = nc * c

    @pl.when(g == 0)
    def _():
        st_ref[...] = jnp.zeros_like(st_ref)

    grp = jnp.where(d == 0, g, ng - 1 - g)

    def head_chunk(hh, r0):
        sl = slice(hh * LANE, (hh + 1) * LANE)
        lb = lb_ref[:, sl]
        qraw = q_ref[0, pl.ds(r0, c), sl]
        z = z_ref[0, pl.ds(r0, c), sl]
        v = v_ref[0, pl.ds(r0, c), sl]
        logf = jnp.log(lb + (1.0 - lb) * jax.nn.sigmoid(z))
        kk = (1.0 - lb) * jax.nn.sigmoid(-z)
        q = qraw * jax.nn.sigmoid(qraw)

        l3 = jnp.concatenate(_split3(logf), axis=0)
        r = _dot(m_ref[0], l3)
        b = r[0:c]
        blast = r[(nlev + 1) * c:(nlev + 1) * c + 1]

        qb = q.astype(BF16)
        kb = kk.astype(BF16)
        sc = mask_ref[0] * _dot_nt(qb, kb)
        for l in range(nlev):
            rl = r[(l + 1) * c:(l + 2) * c]
            up = up_ref[0, l]
            e = jnp.exp((b - rl) * (2.0 * up - 1.0))
            eu = e * up
            qu = (q * eu).astype(BF16)
            kl = (kk * (e - eu)).astype(BF16)
            sc = sc + mask_ref[l + 1] * _dot_nt(qu, kl)

        vb = v.astype(BF16)
        st = st_ref[hh]
        inter = _dot_nt((q * jnp.exp(b)).astype(BF16), st.astype(BF16))
        o = inter + _dot(sc.astype(BF16), vb)
        kd = (kk * jnp.exp(jnp.minimum(blast - b, 0.0))).astype(BF16)
        st_ref[hh] = st * jnp.exp(blast) + _dot(v.T.astype(BF16), kd)
        return o

    def chunk(i, carry):
        ci = jnp.where(d == 0, i, nc - 1 - i)
        r0 = pl.multiple_of(ci * c, c)
        outs = [head_chunk(hh, r0) for hh in range(hpb)]
        row = pl.multiple_of(grp * tg + r0, c)

        @pl.when(d == 0)
        def _():
            for hh in range(hpb):
                of_ref[pl.ds(row, c), hh * LANE:(hh + 1) * LANE] = outs[hh]

        @pl.when(d == 1)
        def _():
            for hh in range(hpb):
                sl = slice(hh * LANE, (hh + 1) * LANE)
                tot = of_ref[pl.ds(row, c), sl] + outs[hh]
                ms = jnp.mean(tot * tot, axis=-1, keepdims=True)
                on = tot * lax.rsqrt(ms + RMS_EPS) * gn_ref[...]
                og = og_ref[0, pl.ds(r0, c), sl]
                o_ref[0, pl.ds(r0, c), sl] = (on * (og * jax.nn.sigmoid(og))).astype(BF16)

        return carry

    lax.fori_loop(0, nc, chunk, 0)


def _hgrn(proj, lb, gn, *, tg, hpb=4):
    bsz, seq, _ = proj.shape
    c = HGRN_CHUNK
    nc = tg // c
    ng = seq // tg
    h = HGRN_HEADS // hpb
    bw = hpb * LANE
    m, up, mask, nlev = _hgrn_consts(c)

    def rows(d, g):
        return jnp.where(d == 0, g, ng - 1 - g)

    def outrows(d, g):
        return jnp.where(d == 0, ng - 1, ng - 1 - g)

    in_specs = [
        pl.BlockSpec((1, tg, bw), lambda b, hh, d, g: (b, rows(d, g), hh)),
        pl.BlockSpec((1, tg, bw), lambda b, hh, d, g: (b, rows(d, g), h * (1 + d) + hh)),
        pl.BlockSpec((1, tg, bw), lambda b, hh, d, g: (b, rows(d, g), 3 * h + hh)),
        pl.BlockSpec((1, tg, bw), lambda b, hh, d, g: (b, outrows(d, g), 4 * h + hh)),
        pl.BlockSpec((1, bw), lambda b, hh, d, g: (0, hh)),
        pl.BlockSpec((1, LANE), lambda b, hh, d, g: (0, 0)),
        pl.BlockSpec((1,) + m.shape[1:], lambda b, hh, d, g: (d, 0, 0)),
        pl.BlockSpec((1,) + up.shape[1:], lambda b, hh, d, g: (d, 0, 0, 0)),
        pl.BlockSpec(mask.shape, lambda b, hh, d, g: (0, 0, 0)),
    ]
    return pl.pallas_call(
        functools.partial(_hgrn_kernel, c=c, nc=nc, nlev=nlev, hpb=hpb),
        grid=(bsz, h, 2, ng),
        in_specs=in_specs,
        out_specs=pl.BlockSpec((1, tg, bw), lambda b, hh, d, g: (b, outrows(d, g), hh)),
        out_shape=jax.ShapeDtypeStruct((bsz, seq, h * bw), BF16),
        scratch_shapes=[pltpu.VMEM((hpb, LANE, HGRN_DK), F32), pltpu.VMEM((seq, bw), F32)],
        compiler_params=_cparams(("parallel", "parallel", "arbitrary", "arbitrary")),
        name="hgrn2_scan",
    )(proj, proj, proj, proj, lb.reshape(1, -1), gn.reshape(1, -1), m, up, mask)


def _shortconv_kernel(*refs, tc):
    (x0_ref, x0p_ref, x0n_ref, x1_ref, x1p_ref, x1n_ref, x2_ref, x2p_ref, x2n_ref,
     w0_ref, w1_ref, w2_ref, b0_ref, b1_ref, b2_ref, u_ref, x0c_ref) = refs
    i = pl.program_id(2)
    n = pl.num_programs(2)
    has_prev = (i > 0).astype(F32)
    has_next = (i < n - 1).astype(F32)
    rid = lax.broadcasted_iota(I32, (tc, LANE), 0)

    def conv(x_ref, xp_ref, xn_ref, w_ref, b_ref):
        x = x_ref[0]
        prev_row = xp_ref[0, SUBLANE - 1:SUBLANE, :] * has_prev
        next_row = xn_ref[0, 0:1, :] * has_next
        xm1 = jnp.where(rid == 0, prev_row, pltpu.roll(x, 1, 0))
        xp1 = jnp.where(rid == tc - 1, next_row, pltpu.roll(x, tc - 1, 0))
        w = w_ref[...]
        return ((b_ref[...] + xm1 * w[0:1]) + x * w[1:2]) + xp1 * w[2:3]

    x0c = conv(x0_ref, x0p_ref, x0n_ref, w0_ref, b0_ref)
    x1c = conv(x1_ref, x1p_ref, x1n_ref, w1_ref, b1_ref)
    vc = conv(x2_ref, x2p_ref, x2n_ref, w2_ref, b2_ref)
    x0c_ref[0] = x0c
    u_ref[0] = vc * x1c


def _shortconv(proj, conv_w, conv_b, *, col0, width, tc):
    bsz, seq, _ = proj.shape
    ncb = width // LANE
    cb0 = col0 // LANE
    nt = seq // tc
    rb = tc // SUBLANE
    nrb = seq // SUBLANE

    def stream(k):
        off = cb0 + k * ncb
        return [
            pl.BlockSpec((1, tc, LANE), lambda b, cc, i, off=off: (b, i, off + cc)),
            pl.BlockSpec((1, SUBLANE, LANE), lambda b, cc, i, off=off: (b, jnp.maximum(i * rb - 1, 0), off + cc)),
            pl.BlockSpec((1, SUBLANE, LANE), lambda b, cc, i, off=off: (b, jnp.minimum((i + 1) * rb, nrb - 1), off + cc)),
        ]

    wspecs = [pl.BlockSpec((3, LANE), lambda b, cc, i, k=k: (0, k * ncb + cc)) for k in range(3)]
    bspecs = [pl.BlockSpec((1, LANE), lambda b, cc, i, k=k: (0, k * ncb + cc)) for k in range(3)]
    ospec = pl.BlockSpec((1, tc, LANE), lambda b, cc, i: (b, i, cc))
    cb2 = conv_b.reshape(1, -1)
    return pl.pallas_call(
        functools.partial(_shortconv_kernel, tc=tc),
        grid=(bsz, ncb, nt),
        in_specs=stream(0) + stream(1) + stream(2) + wspecs + bspecs,
        out_specs=[ospec, ospec],
        out_shape=[jax.ShapeDtypeStruct((bsz, seq, width), F32)] * 2,
        compiler_params=_cparams(("parallel", "parallel", "arbitrary")),
        name="hyena_shortconv",
    )(*([proj] * 9), conv_w, conv_w, conv_w, cb2, cb2, cb2)


def _filter_kernel(band_ref, w1_ref, b1_ref, f1_ref, w2_ref, b2_ref, f2_ref, w3_ref, dec_ref, o_ref,
                   *, tl, seq):
    i = pl.program_id(0)
    pos = (i * tl + lax.broadcasted_iota(I32, (tl, LANE), 0)).astype(F32)
    lane = lax.broadcasted_iota(I32, (tl, LANE), 1)
    t = pos / float(max(seq - 1, 1))
    ang = (2.0 * math.pi / seq) * pos * band_ref[...]
    z = jnp.where(lane == 0, t,
                  jnp.where(lane <= HYENA_BANDS, jnp.cos(ang),
                            jnp.where(lane <= 2 * HYENA_BANDS, -jnp.sin(ang), 0.0)))
    hid = jnp.sin(f1_ref[...] * (jnp.dot(z, w1_ref[...], precision=HIGHEST, preferred_element_type=F32) + b1_ref[...]))
    hid = jnp.sin(f2_ref[...] * (jnp.dot(hid, w2_ref[...], precision=HIGHEST, preferred_element_type=F32) + b2_ref[...]))
    filt = jnp.dot(hid, w3_ref[...], precision=HIGHEST, preferred_element_type=F32)
    o_ref[...] = filt * jnp.exp(-t[:, 0:1] * jnp.abs(dec_ref[...]))


def _hyena_filters(seq, w1, b1, f1, w2, b2, f2, w3, decay, *, tl):
    emb, hid = w1.shape
    wout = w3.shape[1]
    bands = jnp.linspace(1e-4, HYENA_BANDS - 1, HYENA_BANDS, dtype=F32)
    band_row = jnp.zeros((1, LANE), F32).at[0, 1:1 + HYENA_BANDS].set(bands)
    band_row = band_row.at[0, 1 + HYENA_BANDS:1 + 2 * HYENA_BANDS].set(bands)

    def padm(a, r, c):
        return jnp.zeros((r, c), F32).at[:a.shape[0], :a.shape[1]].set(a.astype(F32))

    w1p = padm(w1, LANE, LANE)
    w2p = padm(w2, LANE, LANE)
    w3p = padm(w3, LANE, wout)
    vec = lambda a: padm(a.reshape(1, -1), 1, LANE)
    full = lambda shp: pl.BlockSpec(shp, lambda i: (0, 0))
    return pl.pallas_call(
        functools.partial(_filter_kernel, tl=tl, seq=seq),
        grid=(seq // tl,),
        in_specs=[full((1, LANE)), full((LANE, LANE)), full((1, LANE)), full((1, LANE)),
                  full((LANE, LANE)), full((1, LANE)), full((1, LANE)), full((LANE, wout)), full((1, wout))],
        out_specs=pl.BlockSpec((tl, wout), lambda i: (i, 0)),
        out_shape=jax.ShapeDtypeStruct((seq, wout), F32),
        compiler_params=_cparams(("parallel",)),
        name="hyena_filter_mlp",
    )(band_row, w1p, vec(b1), vec(f1), w2p, vec(b2), vec(f2), w3p, decay.reshape(1, -1).astype(F32))


def _fft_consts(n1):
    n2 = LANE
    n = n1 * n2
    k1 = np.arange(n1)
    f1 = np.exp(-2j * np.pi * np.outer(k1, k1) / n1)
    k2 = np.arange(n2)
    f2 = np.exp(-2j * np.pi * np.outer(k2, k2) / n2)
    tw = np.exp(-2j * np.pi * np.outer(k1, k2) / n)
    f1s = np.concatenate([f1.real, f1.imag], axis=0)
    g2 = np.block([[f2.real, f2.imag], [-f2.imag, f2.real]])
    g2c = np.block([[f2.real, -f2.imag], [f2.imag, f2.real]])
    twc = np.concatenate([tw.real, tw.imag], axis=1)

    def hilo(a):
        a32 = jnp.asarray(a, F32)
        hi = a32.astype(BF16)
        lo = (a32 - hi.astype(F32)).astype(BF16)
        return jnp.stack([hi, lo])

    return dict(f1s=hilo(f1s), g2=hilo(g2), g2c=hilo(g2c), tw=jnp.asarray(twc, F32))


def _mm_const_lhs(c_ref, x, passes):
    xh = x.astype(BF16)
    out = _dot(c_ref[0], xh)
    if passes >= 2:
        out = out + _dot(c_ref[0], (x - xh.astype(F32)).astype(BF16))
    if passes >= 3:
        out = out + _dot(c_ref[1], xh)
    return out


def _mm_const_rhs(x, c_ref, passes):
    xh = x.astype(BF16)
    out = _dot(xh, c_ref[0])
    if passes >= 2:
        out = out + _dot((x - xh.astype(F32)).astype(BF16), c_ref[0])
    if passes >= 3:
        out = out + _dot(xh, c_ref[1])
    return out


def _kfft_kernel(x_ref, f1s_ref, tw_ref, g2_ref, o_ref, *, n1, cb, scale, passes):
    tr = tw_ref[:, 0:LANE]
    ti = tw_ref[:, LANE:2 * LANE]

    def body(cidx, carry):
        x = x_ref[cidx]
        p = _mm_const_lhs(f1s_ref, x, passes)
        ar = p[0:n1]
        ai = p[n1:2 * n1]
        br = ar * tr - ai * ti
        bi = ar * ti + ai * tr
        o_ref[cidx] = _mm_const_rhs(jnp.concatenate([br, bi], axis=1), g2_ref, passes) * scale
        return carry

    lax.fori_loop(0, cb, body, 0, unroll=2)


def _kernel_spectrum(kern_t, consts, *, cb, passes):
    w, n1, _ = kern_t.shape
    full3 = lambda a: pl.BlockSpec(a.shape, lambda i: (0, 0, 0))
    return pl.pallas_call(
        functools.partial(_kfft_kernel, n1=n1, cb=cb, scale=1.0 / (n1 * LANE), passes=passes),
        grid=(w // cb,),
        in_specs=[pl.BlockSpec((cb, n1, LANE), lambda i: (i, 0, 0)),
                  full3(consts["f1s"]),
                  pl.BlockSpec(consts["tw"].shape, lambda i: (0, 0)),
                  full3(consts["g2"])],
        out_specs=pl.BlockSpec((cb, n1, 2 * LANE), lambda i: (i, 0, 0)),
        out_shape=jax.ShapeDtypeStruct((w, n1, 2 * LANE), F32),
        compiler_params=_cparams(("parallel",)),
        name="hyena_filter_fft",
    )(kern_t, consts["f1s"], consts["tw"], consts["g2"])


def _fftconv_kernel(u_ref, kf_ref, f1a_ref, f1b_ref, tw_ref, g2_ref, g2c_ref, o_ref, *, n1, cb, passes):
    n1h = n1 // 2
    tr = tw_ref[:, 0:LANE]
    ti = tw_ref[:, LANE:2 * LANE]

    def body(cidx, carry):
        z = jnp.concatenate([u_ref[0, cidx], u_ref[1, cidx]], axis=1)
        p = _mm_const_lhs(f1a_ref, z, passes)
        ar = p[0:n1, 0:LANE] - p[n1:, LANE:]
        ai = p[0:n1, LANE:] + p[n1:, 0:LANE]
        a2 = jnp.concatenate([ar * tr - ai * ti, ar * ti + ai * tr], axis=1)
        x = _mm_const_rhs(a2, g2_ref, passes)
        xr = x[:, 0:LANE]
        xi = x[:, LANE:]
        kf = kf_ref[cidx]
        kr = kf[:, 0:LANE]
        ki = kf[:, LANE:]
        y = jnp.concatenate([xr * kr - xi * ki, xr * ki + xi * kr], axis=1)
        bm = _mm_const_rhs(y, g2c_ref, passes)
        br = bm[:, 0:LANE]
        bi = bm[:, LANE:]
        b2 = jnp.concatenate([br * tr + bi * ti, bi * tr - br * ti], axis=1)
        p2 = _mm_const_lhs(f1b_ref, b2, passes)
        o_ref[0, cidx] = p2[0:n1h, 0:LANE] + p2[n1h:, LANE:]
        o_ref[1, cidx] = p2[0:n1h, LANE:] - p2[n1h:, 0:LANE]
        return carry

    lax.fori_loop(0, cb, body, 0, unroll=2)


def _fftconv(u_t, kf, consts, *, cb, passes):
    bsz, w, n1h, _ = u_t.shape
    n1 = 2 * n1h
    f1s = consts["f1s"]
    f1a = f1s[:, :, :n1h]
    f1b = jnp.concatenate([f1s[:, 0:n1h, :], f1s[:, n1:n1 + n1h, :]], axis=1)
    full3 = lambda a: pl.BlockSpec(a.shape, lambda i, j: (0, 0, 0))
    return pl.pallas_call(
        functools.partial(_fftconv_kernel, n1=n1, cb=cb, passes=passes),
        grid=(bsz // 2, w // cb),
        in_specs=[pl.BlockSpec((2, cb, n1h, LANE), lambda i, j: (i, j, 0, 0)),
                  pl.BlockSpec((cb, n1, 2 * LANE), lambda i, j: (j, 0, 0)),
                  full3(f1a), full3(f1b),
                  pl.BlockSpec(consts["tw"].shape, lambda i, j: (0, 0)),
                  full3(consts["g2"]), full3(consts["g2c"])],
        out_specs=pl.BlockSpec((2, cb, n1h, LANE), lambda i, j: (i, j, 0, 0)),
        out_shape=jax.ShapeDtypeStruct(u_t.shape, F32),
        compiler_params=_cparams(("parallel", "parallel")),
        name="hyena_fftconv",
    )(u_t, kf, f1a, f1b, consts["tw"], consts["g2"], consts["g2c"])


def _merge_kernel(oh_ref, yc_ref, u_ref, x0_ref, ga_ref, gb_ref, x_ref, hb_ref, wa_ref, wb_ref, wo_ref, o_ref):
    ya = _dot(oh_ref[...], wa_ref[...])
    ybp = x0_ref[...] * (yc_ref[...] + u_ref[...] * hb_ref[...])
    yb = _dot(ybp.astype(BF16), wb_ref[...])
    merged = jax.nn.sigmoid(ga_ref[...]) * ya + jax.nn.sigmoid(gb_ref[...]) * yb
    o_ref[...] = x_ref[...] + _dot(merged.astype(BF16), wo_ref[...])


def _merge(oh, yc, u, x0c, proj2, x2, hbias, wa, wb, wo, *, gate_col0, tm):
    n, d = x2.shape
    gc = gate_col0 // d
    row = lambda c=0: pl.BlockSpec((tm, d), lambda i, c=c: (i, c))
    wfull = pl.BlockSpec((d, d), lambda i: (0, 0))
    return pl.pallas_call(
        _merge_kernel,
        grid=(n // tm,),
        in_specs=[row(), row(), row(), row(), row(gc), row(gc + 1), row(),
                  pl.BlockSpec((1, d), lambda i: (0, 0)), wfull, wfull, wfull],
        out_specs=row(),
        out_shape=jax.ShapeDtypeStruct((n, d), F32),
        compiler_params=_cparams(("parallel",)),
        name="branch_merge",
    )(oh, yc, u, x0c, proj2, proj2, x2, hbias.reshape(1, d), wa, wb, wo)


def _extract_topk(s, k, vals_ref, idx_ref, ids=None):
    nrow = s.shape[0]
    rid = lax.broadcasted_iota(I32, s.shape, 0)
    for j in range(k):
        m = jnp.max(s, axis=0, keepdims=True)
        ix = jnp.min(jnp.where(s == m, rid, nrow), axis=0, keepdims=True)
        hit = rid == ix
        vals_ref[j:j + 1, :] = m
        if ids is None:
            idx_ref[j:j + 1, :] = ix
        else:
            idx_ref[j:j + 1, :] = jnp.max(jnp.where(hit, ids, -1), axis=0, keepdims=True)
        s = jnp.where(hit, -jnp.inf, s)


def _peer_topk_kernel(q_ref, sk_ref, e_ref, g_ref, par_ref, s1_ref, i1_ref, s2_ref, i2_ref, ts_ref, te_ref):
    k = PEER_TOPK
    nk = PEER_NKEYS
    q = q_ref[...]
    s_a = _dot_nt(sk_ref[0, 0], q[:, 0:nk].astype(BF16))
    s_b = _dot_nt(sk_ref[0, 1], q[:, nk:2 * nk].astype(BF16))
    _extract_topk(s_a, k, s1_ref, i1_ref)
    _extract_topk(s_b, k, s2_ref, i2_ref)
    s1 = s1_ref[...]
    i1 = i1_ref[...]
    s2 = s2_ref[...]
    i2 = i2_ref[...]
    rid8 = lax.broadcasted_iota(I32, (SUBLANE, s1.shape[1]), 0)
    cs = [s1[0:1] + s2]
    ci = [i1[0:1] * nk + i2]
    for a in range(1, SUBLANE):
        nb = k // (a + 1)
        cs.append(jnp.where(rid8 < nb, s1[a:a + 1] + s2[0:SUBLANE], -jnp.inf))
        ci.append(i1[a:a + 1] * nk + i2[0:SUBLANE])
    cs.append(s1[SUBLANE:k] + s2[0:1])
    ci.append(i1[SUBLANE:k] * nk + i2[0:1])
    _extract_topk(jnp.concatenate(cs, axis=0), k, ts_ref, te_ref, ids=jnp.concatenate(ci, axis=0))
    ts = ts_ref[...]
    ex = jnp.exp(ts - ts[0:1])
    g_ref[0] = ex / jnp.sum(ex, axis=0, keepdims=True)
    te = te_ref[...]
    e_ref[0] = te >> 1
    par_ref[0] = (te & 1).astype(F32)


def _peer_topk(qp, subkeys_bf16, *, tt):
    n = qp.shape[0]
    h = PEER_HEADS
    k = PEER_TOPK
    nk = PEER_NKEYS
    ospec = pl.BlockSpec((1, k, tt), lambda i, hh: (hh, 0, i))
    return pl.pallas_call(
        _peer_topk_kernel,
        grid=(n // tt, h),
        in_specs=[pl.BlockSpec((tt, 2 * nk), lambda i, hh: (i, hh)),
                  pl.BlockSpec((1, 2, nk, nk), lambda i, hh: (hh, 0, 0, 0))],
        out_specs=[ospec, ospec, ospec],
        out_shape=[jax.ShapeDtypeStruct((h, k, n), I32), jax.ShapeDtypeStruct((h, k, n), F32),
                   jax.ShapeDtypeStruct((h, k, n), F32)],
        scratch_shapes=[pltpu.VMEM((k, tt), F32), pltpu.VMEM((k, tt), I32),
                        pltpu.VMEM((k, tt), F32), pltpu.VMEM((k, tt), I32),
                        pltpu.VMEM((k, tt), F32), pltpu.VMEM((k, tt), I32)],
        compiler_params=_cparams(("parallel", "arbitrary")),
        name="peer_topk",
    )(qp, subkeys_bf16)


def _lane_to_rows(row):
    return jnp.broadcast_to(row, (LANE, LANE)).T


def _sublane_sums8(ps, sub):
    lo4 = sub < 4
    c = []
    for i in range(4):
        a = jnp.where(lo4, ps[i], ps[i + 4])
        b = jnp.where(lo4, ps[i + 4], ps[i])
        c.append(a + pltpu.roll(b, 4, 0))
    m2 = (sub & 3) < 2
    dd = []
    for i in range(2):
        x = jnp.where(m2, c[i], pltpu.roll(c[i + 2], 2, 0))
        y = jnp.where(m2, pltpu.roll(c[i], 6, 0), c[i + 2])
        dd.append(x + y)
    m1 = (sub & 1) < 1
    x = jnp.where(m1, dd[0], pltpu.roll(dd[1], 1, 0))
    y = jnp.where(m1, pltpu.roll(dd[0], 7, 0), dd[1])
    return x + y


PEER_U_TOKENS_PER_ITER = 16


def _pack_expert_pairs(tab):
    ne, d = tab.shape
    bits = lax.bitcast_convert_type(tab.astype(BF16), jnp.uint16).astype(jnp.uint32).reshape(ne // 2, 2, d)
    word = (bits[:, 0] << 16) | bits[:, 1]
    return lax.bitcast_convert_type(word, I32).reshape(ne // 2, SUBLANE, LANE)


def _peer_u_kernel(idx_ref, x_ref, g_ref, par_ref, tab_ref, alo_ref, ahi_ref, rall_ref, sh_ref, *, tb, npair):
    sub = lax.broadcasted_iota(I32, (SUBLANE, LANE), 0)
    upper = jnp.int32(-65536)
    nu = PEER_U_TOKENS_PER_ITER

    def one_token(t, u):
        x = x_ref[t]
        par = par_ref[pl.ds(t, 1), :]
        sh_ref[u] = (16.0 * _lane_to_rows(par)).astype(I32)
        for gi in range(npair // SUBLANE):
            ps = []
            for i in range(SUBLANE):
                j = gi * SUBLANE + i
                w = tab_ref[idx_ref[t, j]]
                sh = jnp.broadcast_to(sh_ref[u, j:j + 1, :], (SUBLANE, LANE))
                ps.append(x * lax.bitcast_convert_type(lax.shift_left(w, sh) & upper, F32))
            rall_ref[u, gi * SUBLANE:(gi + 1) * SUBLANE, :] = _sublane_sums8(ps, sub)
        r = rall_ref[u]
        s = jnp.sum(r.T, axis=0, keepdims=True)
        a = jax.nn.gelu(s, approximate=True) * g_ref[pl.ds(t, 1), :]
        ahi = a * par
        ahi_ref[pl.ds(t, 1), :] = ahi
        alo_ref[pl.ds(t, 1), :] = a - ahi

    def tokens(i, carry):
        for u in range(nu):
            one_token(i * nu + u, u)
        return carry

    lax.fori_loop(0, tb // nu, tokens, 0)


def _peer_u(idx, x3, g, par, tab, *, tb):
    n, npair = idx.shape
    assert npair == LANE and tb % PEER_U_TOKENS_PER_ITER == 0
    vrow = lambda: pl.BlockSpec((tb, npair), lambda i: (i, 0))
    return pl.pallas_call(
        functools.partial(_peer_u_kernel, tb=tb, npair=npair),
        grid=(n // tb,),
        in_specs=[pl.BlockSpec((tb, npair), lambda i: (i, 0), memory_space=pltpu.SMEM),
                  pl.BlockSpec((tb, SUBLANE, LANE), lambda i: (i, 0, 0)),
                  vrow(), vrow(),
                  pl.BlockSpec(tab.shape, lambda i: (0, 0, 0), pipeline_mode=pl.Buffered(1))],
        out_specs=[vrow(), vrow()],
        out_shape=[jax.ShapeDtypeStruct((n, npair), F32)] * 2,
        scratch_shapes=[pltpu.VMEM((PEER_U_TOKENS_PER_ITER, npair, LANE), F32),
                        pltpu.VMEM((PEER_U_TOKENS_PER_ITER, npair, LANE), I32)],
        compiler_params=_cparams(("parallel",)),
        name="peer_expert_in",
    )(idx, x3, g, par, tab)


def _peer_v_consts(npair):
    rows = 2 * SUBLANE
    col = np.arange(npair * rows)
    expand = (col[None, :] // rows) == np.arange(npair)[:, None]
    s = np.arange(SUBLANE)[:, None]
    plo = (col[None, :] % rows) == s
    phi = (col[None, :] % rows) == s + SUBLANE
    return jnp.asarray(expand, BF16), jnp.asarray(plo, F32), jnp.asarray(phi, F32)


def _peer_v_kernel(idx_ref, alo_ref, ahi_ref, h_ref, gf_ref, ex_ref, plo_ref, phi_ref, tab_ref, o_ref,
                   elo_ref, ehi_ref, *, tb, npair):
    half = npair // 2
    kh = half * 2 * SUBLANE

    def lane_expand(a):
        ah = a.astype(BF16)
        al = (a - ah.astype(F32)).astype(BF16)
        return _dot(ah, ex_ref[...]) + _dot(al, ex_ref[...])

    elo_ref[...] = lane_expand(alo_ref[...])
    ehi_ref[...] = lane_expand(ahi_ref[...])

    def token(t, carry):
        tiles = [tab_ref[idx_ref[t, j]] for j in range(npair)]
        rhs = jnp.concatenate([jnp.concatenate(tiles[:half], axis=0),
                               jnp.concatenate(tiles[half:], axis=0)], axis=1)
        coef = elo_ref[pl.ds(t, 1), :] * plo_ref[...] + ehi_ref[pl.ds(t, 1), :] * phi_ref[...]
        c2 = jnp.concatenate([coef[:, :kh], coef[:, kh:]], axis=0)
        ch = c2.astype(BF16)
        cl = (c2 - ch.astype(F32)).astype(BF16)
        r = _dot(jnp.concatenate([ch, cl], axis=0), rhs)
        acc = ((r[0:SUBLANE, 0:LANE] + r[SUBLANE:2 * SUBLANE, LANE:])
               + (r[2 * SUBLANE:3 * SUBLANE, 0:LANE] + r[3 * SUBLANE:, LANE:]))
        hh = h_ref[t] + acc
        ss = jnp.sum(jnp.sum(hh * hh, axis=1, keepdims=True), axis=0, keepdims=True)
        o_ref[t] = hh * lax.rsqrt(ss * (1.0 / (SUBLANE * LANE)) + RMS_EPS) * gf_ref[...]
        return carry

    lax.fori_loop(0, tb, token, 0, unroll=16)


def _peer_v(idx, alo, ahi, h3, gfin, tab, *, tb):
    n, npair = idx.shape
    assert npair == LANE
    ex, plo, phi = _peer_v_consts(npair)
    vrow = lambda: pl.BlockSpec((tb, npair), lambda i: (i, 0))
    full = lambda a: pl.BlockSpec(a.shape, lambda i: (0, 0))
    return pl.pallas_call(
        functools.partial(_peer_v_kernel, tb=tb, npair=npair),
        grid=(n // tb,),
        in_specs=[pl.BlockSpec((tb, npair), lambda i: (i, 0), memory_space=pltpu.SMEM),
                  vrow(), vrow(),
                  pl.BlockSpec((tb, SUBLANE, LANE), lambda i: (i, 0, 0)),
                  pl.BlockSpec((SUBLANE, LANE), lambda i: (0, 0)),
                  full(ex), full(plo), full(phi),
                  pl.BlockSpec(tab.shape, lambda i: (0, 0, 0), pipeline_mode=pl.Buffered(1))],
        out_specs=pl.BlockSpec((tb, SUBLANE, LANE), lambda i: (i, 0, 0)),
        out_shape=jax.ShapeDtypeStruct(h3.shape, F32),
        scratch_shapes=[pltpu.VMEM((tb, 2 * SUBLANE * npair), F32), pltpu.VMEM((tb, 2 * SUBLANE * npair), F32)],
        compiler_params=_cparams(("parallel",)),
        name="peer_expert_out",
    )(idx, alo, ahi, h3, gfin, ex, plo, phi, tab)


def _pick(n, pref):
    t = min(n, pref)
    while n % t:
        t //= 2
    return t


def _layer(h, lb, p, fft_passes):
    bsz, seq, d = h.shape
    n = bsz * seq
    kw = HGRN_HEADS * HGRN_DK
    hyena_col0 = 3 * kw + 2 * d
    gate_col0 = hyena_col0 + 3 * d

    proj2 = _norm_matmul(h.reshape(n, d), p["norm_mix_g"], p["w_in"].astype(BF16),
                         tm=_pick(n, 1024), tn=1024)
    proj = proj2.reshape(bsz, seq, -1)

    oh = _hgrn(proj, lb, p["hgrn_norm_g"], tg=_pick(seq, 1024))

    u, x0c = _shortconv(proj, p["hyena_conv_w"], p["hyena_conv_b"], col0=hyena_col0, width=d,
                        tc=_pick(seq, 1024))
    filt = _hyena_filters(seq, p["filt_w1"], p["filt_b1"], p["filt_freq1"], p["filt_w2"], p["filt_b2"],
                          p["filt_freq2"], p["filt_w3"], p["filt_decay"], tl=_pick(seq, 512))
    h_f, h_b = filt[:, :d], filt[:, d:]
    kern = jnp.concatenate([h_f, jnp.zeros_like(h_f[:1]), h_b[1:][::-1]], axis=0)
    n1 = 2 * seq // LANE
    consts = _fft_consts(n1)
    kf = _kernel_spectrum(kern.T.reshape(d, n1, LANE), consts, cb=_pick(d, 8), passes=3)
    u_t = jnp.swapaxes(u, 1, 2).reshape(bsz, d, n1 // 2, LANE)
    y_t = _fftconv(u_t, kf, consts, cb=_pick(d, 16), passes=fft_passes)
    yc = jnp.swapaxes(y_t.reshape(bsz, d, seq), 1, 2)

    h1 = _merge(oh.reshape(n, d), yc.reshape(n, d), u.reshape(n, d), x0c.reshape(n, d), proj2,
                h.reshape(n, d), p["hyena_bias"], p["w_branch_a"].astype(BF16),
                p["w_branch_b"].astype(BF16), p["w_out"].astype(BF16), gate_col0=gate_col0, tm=_pick(n, 256))

    qp, hn = _norm_matmul(h1, p["norm_ffn_g"], p["peer_w_q"].astype(BF16), tm=_pick(n, 1024),
                          tn=_pick(p["peer_w_q"].shape[1], 1024), emit_xn=True)
    tiles, gates, halves = _peer_topk(qp, p["peer_subkeys"].astype(BF16), tt=_pick(n, 256))
    npair = PEER_HEADS * PEER_TOPK
    tokmajor = lambda a: a.transpose(2, 0, 1).reshape(n, npair)
    idx = tokmajor(tiles)
    ne = p["peer_u"].shape[0]
    utab = _pack_expert_pairs(p["peer_u"])
    vtab = p["peer_v"].astype(BF16).reshape(ne // 2, 2 * SUBLANE, LANE)
    tb = _pick(n, 64)
    alo, ahi = _peer_u(idx, hn.reshape(n, SUBLANE, LANE), tokmajor(gates), tokmajor(halves), utab, tb=tb)
    return h1, idx, alo, ahi, vtab, tb


def kernel(x, norm_mix_g, w_in, hgrn_lb_logits, hgrn_norm_g, hyena_conv_w, hyena_conv_b, filt_w1, filt_b1, filt_freq1, filt_w2, filt_b2, filt_freq2, filt_w3, filt_decay, hyena_bias, w_branch_a, w_branch_b, w_out, norm_ffn_g, peer_w_q, peer_subkeys, peer_u, peer_v, norm_final_g):
    bsz, seq, d = x.shape
    n = bsz * seq
    depth = w_in.shape[0]
    assert depth == 1, "the fused final norm assumes a single layer"
    lb_table = jnp.cumsum(jax.nn.softmax(hgrn_lb_logits.astype(F32), axis=0), axis=0)
    p = dict(norm_mix_g=norm_mix_g[0], w_in=w_in[0], hgrn_norm_g=hgrn_norm_g[0], hyena_conv_w=hyena_conv_w[0],
             hyena_conv_b=hyena_conv_b[0], filt_w1=filt_w1[0], filt_b1=filt_b1[0], filt_freq1=filt_freq1[0],
             filt_w2=filt_w2[0], filt_b2=filt_b2[0], filt_freq2=filt_freq2[0], filt_w3=filt_w3[0],
             filt_decay=filt_decay[0], hyena_bias=hyena_bias[0], w_branch_a=w_branch_a[0],
             w_branch_b=w_branch_b[0], w_out=w_out[0], norm_ffn_g=norm_ffn_g[0], peer_w_q=peer_w_q[0],
             peer_subkeys=peer_subkeys[0], peer_u=peer_u[0], peer_v=peer_v[0])
    h1, idx, alo, ahi, vtab, tb = _layer(x, lb_table[0], p, fft_passes=1)
    out = _peer_v(idx, alo, ahi, h1.reshape(n, SUBLANE, LANE), norm_final_g.reshape(SUBLANE, LANE), vtab, tb=tb)
    return out.reshape(bsz, seq, d)
```

```python
import functools
import math

import numpy as np
import jax
import jax.numpy as jnp
from jax import lax
from jax.experimental import pallas as pl
from jax.experimental.pallas import tpu as pltpu

F32 = jnp.float32
BF16 = jnp.bfloat16
I32 = jnp.int32

RMS_EPS = 1e-6
LANE = 128
SUBLANE = 8
VMEM_LIMIT = 56 * 1024 * 1024

HGRN_HEADS = 8
HGRN_DK = 128
HGRN_CHUNK = 128
HYENA_BANDS = 16
HYENA_HIDDEN = 64
PEER_HEADS = 8
PEER_NKEYS = 128
PEER_TOPK = 16

HIGHEST = lax.Precision.HIGHEST


def _cparams(sem, vmem=VMEM_LIMIT):
    return pltpu.CompilerParams(dimension_semantics=sem, vmem_limit_bytes=vmem)


def _dot(a, b):
    return jnp.dot(a, b, preferred_element_type=F32)


def _dot_nt(a, b):
    return lax.dot_general(a, b, (((1,), (1,)), ((), ())), preferred_element_type=F32)


def _normmm_kernel(x_ref, g_ref, w_ref, o_ref, *rest, emit_xn):
    if emit_xn:
        xn_out_ref, xn_ref = rest
    else:
        (xn_ref,) = rest

    @pl.when(pl.program_id(1) == 0)
    def _():
        x = x_ref[...]
        ms = jnp.mean(x * x, axis=-1, keepdims=True)
        xn = x * lax.rsqrt(ms + RMS_EPS) * g_ref[...]
        xn_ref[...] = xn.astype(BF16)
        if emit_xn:
            xn_out_ref[...] = xn

    o_ref[...] = _dot(xn_ref[...], w_ref[...])


def _norm_matmul(x, g, w_bf16, *, tm, tn, emit_xn=False):
    n, d = x.shape
    nout = w_bf16.shape[1]
    out_shape = [jax.ShapeDtypeStruct((n, nout), F32)]
    out_specs = [pl.BlockSpec((tm, tn), lambda i, j: (i, j))]
    if emit_xn:
        out_shape.append(jax.ShapeDtypeStruct((n, d), F32))
        out_specs.append(pl.BlockSpec((tm, d), lambda i, j: (i, 0)))
    res = pl.pallas_call(
        functools.partial(_normmm_kernel, emit_xn=emit_xn),
        grid=(n // tm, nout // tn),
        in_specs=[pl.BlockSpec((tm, d), lambda i, j: (i, 0)),
                  pl.BlockSpec((1, d), lambda i, j: (0, 0)),
                  pl.BlockSpec((d, tn), lambda i, j: (0, j))],
        out_specs=out_specs,
        out_shape=out_shape,
        scratch_shapes=[pltpu.VMEM((tm, d), BF16)],
        compiler_params=_cparams(("parallel", "arbitrary")),
        name="norm_matmul",
    )(x, g.reshape(1, d), w_bf16)
    return res if emit_xn else res[0]


def _hgrn_consts(c):
    nlev = int(math.log2(c))
    t = np.arange(c)
    m = np.zeros((2, (nlev + 2) * c, c), np.float32)
    up = np.zeros((2, nlev, c, LANE), np.float32)
    mask = np.zeros((nlev + 1, c, c), np.float32)
    for d in range(2):
        p = t if d == 0 else c - 1 - t
        m[d, 0:c] = p[None, :] <= p[:, None]
        for l in range(nlev):
            hs = 1 << l
            pmid = (p // (2 * hs)) * (2 * hs) + hs - 1
            m[d, (l + 1) * c:(l + 2) * c] = p[None, :] <= pmid[:, None]
            up[d, l] = (((p // hs) % 2) == 1)[:, None]
        m[d, (nlev + 1) * c:] = 1.0
    mask[0] = np.eye(c)
    for l in range(nlev):
        hs = 1 << l
        mask[l + 1] = (t[:, None] // (2 * hs)) == (t[None, :] // (2 * hs))
    m3 = np.concatenate([m, m, m], axis=2)
    return jnp.asarray(m3, BF16), jnp.asarray(up, F32), jnp.asarray(mask, F32), nlev


def _split3(x):
    hi = x.astype(BF16)
    r1 = x - hi.astype(F32)
    mid = r1.astype(BF16)
    lo = (r1 - mid.astype(F32)).astype(BF16)
    return hi, mid, lo


def _hgrn_kernel(q_ref, z_ref, v_ref, og_ref, lb_ref, gn_ref, m_ref, up_ref, mask_ref,
                 o_ref, st_ref, of_ref, *, c, nc, nlev, hpb):
    d = pl.program_id(2)
    g = pl.program_id(3)
    ng = pl.num_programs(3)
    tg = nc * c

    @pl.when(g == 0)
    def _():
        st_ref[...] = jnp.zeros_like(st_ref)

    grp = jnp.where(d == 0, g, ng - 1 - g)

    def head_chunk(hh, r0):
        sl = slice(hh * LANE, (hh + 1) * LANE)
        lb = lb_ref[:, sl]
        qraw = q_ref[0, pl.ds(r0, c), sl]
        z = z_ref[0, pl.ds(r0, c), sl]
        v = v_ref[0, pl.ds(r0, c), sl]
        logf = jnp.log(lb + (1.0 - lb) * jax.nn.sigmoid(z))
        kk = (1.0 - lb) * jax.nn.sigmoid(-z)
        q = qraw * jax.nn.sigmoid(qraw)

        l3 = jnp.concatenate(_split3(logf), axis=0)
        r = _dot(m_ref[0], l3)
        b = r[0:c]
        blast = r[(nlev + 1) * c:(nlev + 1) * c + 1]

        qb = q.astype(BF16)
        kb = kk.astype(BF16)
        sc = mask_ref[0] * _dot_nt(qb, kb)
        for l in range(nlev):
            rl = r[(l + 1) * c:(l + 2) * c]
            up = up_ref[0, l]
            e = jnp.exp((b - rl) * (2.0 * up - 1.0))
            eu = e * up
            qu = (q * eu).astype(BF16)
            kl = (kk * (e - eu)).astype(BF16)
            sc = sc + mask_ref[l + 1] * _dot_nt(qu, kl)

        vb = v.astype(BF16)
        st = st_ref[hh]
        inter = _dot_nt((q * jnp.exp(b)).astype(BF16), st.astype(BF16))
        o = inter + _dot(sc.astype(BF16), vb)
        kd = (kk * jnp.exp(jnp.minimum(blast - b, 0.0))).astype(BF16)
        st_ref[hh] = st * jnp.exp(blast) + _dot(v.T.astype(BF16), kd)
        return o

    def chunk(i, carry):
        ci = jnp.where(d == 0, i, nc - 1 - i)
        r0 = pl.multiple_of(ci * c, c)
        outs = [head_chunk(hh, r0) for hh in range(hpb)]
        row = pl.multiple_of(grp * tg + r0, c)

        @pl.when(d == 0)
        def _():
            for hh in range(hpb):
                of_ref[pl.ds(row, c), hh * LANE:(hh + 1) * LANE] = outs[hh]

        @pl.when(d == 1)
        def _():
            for hh in range(hpb):
                sl = slice(hh * LANE, (hh + 1) * LANE)
                tot = of_ref[pl.ds(row, c), sl] + outs[hh]
                ms = jnp.mean(tot * tot, axis=-1, keepdims=True)
                on = tot * lax.rsqrt(ms + RMS_EPS) * gn_ref[...]
                og = og_ref[0, pl.ds(r0, c), sl]
                o_ref[0, pl.ds(r0, c), sl] = (on * (og * jax.nn.sigmoid(og))).astype(BF16)

        return carry

    lax.fori_loop(0, nc, chunk, 0)


def _hgrn(proj, lb, gn, *, tg, hpb=4):
    bsz, seq, _ = proj.shape
    c = HGRN_CHUNK
    nc = tg // c
    ng = seq // tg
    h = HGRN_HEADS // hpb
    bw = hpb * LANE
    m, up, mask, nlev = _hgrn_consts(c)

    def rows(d, g):
        return jnp.where(d == 0, g, ng - 1 - g)

    def outrows(d, g):
        return jnp.where(d == 0, ng - 1, ng - 1 - g)

    in_specs = [
        pl.BlockSpec((1, tg, bw), lambda b, hh, d, g: (b, rows(d, g), hh)),
        pl.BlockSpec((1, tg, bw), lambda b, hh, d, g: (b, rows(d, g), h * (1 + d) + hh)),
        pl.BlockSpec((1, tg, bw), lambda b, hh, d, g: (b, rows(d, g), 3 * h + hh)),
        pl.BlockSpec((1, tg, bw), lambda b, hh, d, g: (b, outrows(d, g), 4 * h + hh)),
        pl.BlockSpec((1, bw), lambda b, hh, d, g: (0, hh)),
        pl.BlockSpec((1, LANE), lambda b, hh, d, g: (0, 0)),
        pl.BlockSpec((1,) + m.shape[1:], lambda b, hh, d, g: (d, 0, 0)),
        pl.BlockSpec((1,) + up.shape[1:], lambda b, hh, d, g: (d, 0, 0, 0)),
        pl.BlockSpec(mask.shape, lambda b, hh, d, g: (0, 0, 0)),
    ]
    return pl.pallas_call(
        functools.partial(_hgrn_kernel, c=c, nc=nc, nlev=nlev, hpb=hpb),
        grid=(bsz, h, 2, ng),
        in_specs=in_specs,
        out_specs=pl.BlockSpec((1, tg, bw), lambda b, hh, d, g: (b, outrows(d, g), hh)),
        out_shape=jax.ShapeDtypeStruct((bsz, seq, h * bw), BF16),
        scratch_shapes=[pltpu.VMEM((hpb, LANE, HGRN_DK), F32), pltpu.VMEM((seq, bw), F32)],
        compiler_params=_cparams(("parallel", "parallel", "arbitrary", "arbitrary")),
        name="hgrn2_scan",
    )(proj, proj, proj, proj, lb.reshape(1, -1), gn.reshape(1, -1), m, up, mask)


def _shortconv_kernel(*refs, tc):
    (x0_ref, x0p_ref, x0n_ref, x1_ref, x1p_ref, x1n_ref, x2_ref, x2p_ref, x2n_ref,
     w0_ref, w1_ref, w2_ref, b0_ref, b1_ref, b2_ref, u_ref, x0c_ref) = refs
    i = pl.program_id(2)
    n = pl.num_programs(2)
    has_prev = (i > 0).astype(F32)
    has_next = (i < n - 1).astype(F32)
    rid = lax.broadcasted_iota(I32, (tc, LANE), 0)

    def conv(x_ref, xp_ref, xn_ref, w_ref, b_ref):
        x = x_ref[0]
        prev_row = xp_ref[0, SUBLANE - 1:SUBLANE, :] * has_prev
        next_row = xn_ref[0, 0:1, :] * has_next
        xm1 = jnp.where(rid == 0, prev_row, pltpu.roll(x, 1, 0))
        xp1 = jnp.where(rid == tc - 1, next_row, pltpu.roll(x, tc - 1, 0))
        w = w_ref[...]
        return ((b_ref[...] + xm1 * w[0:1]) + x * w[1:2]) + xp1 * w[2:3]

    x0c = conv(x0_ref, x0p_ref, x0n_ref, w0_ref, b0_ref)
    x1c = conv(x1_ref, x1p_ref, x1n_ref, w1_ref, b1_ref)
    vc = conv(x2_ref, x2p_ref, x2n_ref, w2_ref, b2_ref)
    x0c_ref[0] = x0c
    u_ref[0] = vc * x1c


def _shortconv(proj, conv_w, conv_b, *, col0, width, tc):
    bsz, seq, _ = proj.shape
    ncb = width // LANE
    cb0 = col0 // LANE
    nt = seq // tc
    rb = tc // SUBLANE
    nrb = seq // SUBLANE

    def stream(k):
        off = cb0 + k * ncb
        return [
            pl.BlockSpec((1, tc, LANE), lambda b, cc, i, off=off: (b, i, off + cc)),
            pl.BlockSpec((1, SUBLANE, LANE), lambda b, cc, i, off=off: (b, jnp.maximum(i * rb - 1, 0), off + cc)),
            pl.BlockSpec((1, SUBLANE, LANE), lambda b, cc, i, off=off: (b, jnp.minimum((i + 1) * rb, nrb - 1), off + cc)),
        ]

    wspecs = [pl.BlockSpec((3, LANE), lambda b, cc, i, k=k: (0, k * ncb + cc)) for k in range(3)]
    bspecs = [pl.BlockSpec((1, LANE), lambda b, cc, i, k=k: (0, k * ncb + cc)) for k in range(3)]
    ospec = pl.BlockSpec((1, tc, LANE), lambda b, cc, i: (b, i, cc))
    cb2 = conv_b.reshape(1, -1)
    return pl.pallas_call(
        functools.partial(_shortconv_kernel, tc=tc),
        grid=(bsz, ncb, nt),
        in_specs=stream(0) + stream(1) + stream(2) + wspecs + bspecs,
        out_specs=[ospec, ospec],
        out_shape=[jax.ShapeDtypeStruct((bsz, seq, width), F32)] * 2,
        compiler_params=_cparams(("parallel", "parallel", "arbitrary")),
        name="hyena_shortconv",
    )(*([proj] * 9), conv_w, conv_w, conv_w, cb2, cb2, cb2)


def _filter_kernel(band_ref, w1_ref, b1_ref, f1_ref, w2_ref, b2_ref, f2_ref, w3_ref, dec_ref, o_ref,
                   *, tl, seq):
    i = pl.program_id(0)
    n = i * tl + lax.broadcasted_iota(I32, (tl, LANE), 0)
    pos = jnp.where(n < seq, n, 2 * seq - n).astype(F32)
    lane = lax.broadcasted_iota(I32, (tl, LANE), 1)
    t = pos / float(max(seq - 1, 1))
    ang = (2.0 * math.pi / seq) * pos * band_ref[...]
    z = jnp.where(lane == 0, t,
                  jnp.where(lane <= HYENA_BANDS, jnp.cos(ang),
                            jnp.where(lane <= 2 * HYENA_BANDS, -jnp.sin(ang), 0.0)))
    hid = jnp.sin(f1_ref[...] * (jnp.dot(z, w1_ref[...], precision=HIGHEST, preferred_element_type=F32) + b1_ref[...]))
    hid = jnp.sin(f2_ref[...] * (jnp.dot(hid, w2_ref[...], precision=HIGHEST, preferred_element_type=F32) + b2_ref[...]))
    filt = jnp.dot(hid, w3_ref[...], precision=HIGHEST, preferred_element_type=F32)
    filt = filt * jnp.exp(-t[:, 0:1] * jnp.abs(dec_ref[...]))
    o_ref[...] = jnp.where(n[:, 0:1] == seq, 0.0, filt)


def _hyena_conv_kernel(seq, w1, b1, f1, w2, b2, f2, w3, decay, *, tl):
    emb, hid = w1.shape
    wout = w3.shape[1] // 2
    nb = seq // tl
    bands = jnp.linspace(1e-4, HYENA_BANDS - 1, HYENA_BANDS, dtype=F32)
    band_row = jnp.zeros((1, LANE), F32).at[0, 1:1 + HYENA_BANDS].set(bands)
    band_row = band_row.at[0, 1 + HYENA_BANDS:1 + 2 * HYENA_BANDS].set(bands)

    def padm(a, r, c):
        return jnp.zeros((r, c), F32).at[:a.shape[0], :a.shape[1]].set(a.astype(F32))

    w1p = padm(w1, LANE, LANE)
    w2p = padm(w2, LANE, LANE)
    w3p = padm(w3, LANE, 2 * wout)
    vec = lambda a: padm(a.reshape(1, -1), 1, LANE)
    full = lambda shp: pl.BlockSpec(shp, lambda i: (0, 0))
    half = lambda rows: pl.BlockSpec((rows, wout), lambda i: (0, i // nb))
    return pl.pallas_call(
        functools.partial(_filter_kernel, tl=tl, seq=seq),
        grid=(2 * nb,),
        in_specs=[full((1, LANE)), full((LANE, LANE)), full((1, LANE)), full((1, LANE)),
                  full((LANE, LANE)), full((1, LANE)), full((1, LANE)), half(LANE), half(1)],
        out_specs=pl.BlockSpec((tl, wout), lambda i: (i, 0)),
        out_shape=jax.ShapeDtypeStruct((2 * seq, wout), F32),
        compiler_params=_cparams(("parallel",)),
        name="hyena_filter_mlp",
    )(band_row, w1p, vec(b1), vec(f1), w2p, vec(b2), vec(f2), w3p, decay.reshape(1, -1).astype(F32))


def _fft_consts(n1):
    n2 = LANE
    n = n1 * n2
    k1 = np.arange(n1)
    f1 = np.exp(-2j * np.pi * np.outer(k1, k1) / n1)
    k2 = np.arange(n2)
    f2 = np.exp(-2j * np.pi * np.outer(k2, k2) / n2)
    tw = np.exp(-2j * np.pi * np.outer(k1, k2) / n)
    f1s = np.concatenate([f1.real, f1.imag], axis=0)
    g2 = np.block([[f2.real, f2.imag], [-f2.imag, f2.real]])
    g2c = np.block([[f2.real, -f2.imag], [f2.imag, f2.real]])
    twc = np.concatenate([tw.real, tw.imag], axis=1)

    def hilo(a):
        a32 = jnp.asarray(a, F32)
        hi = a32.astype(BF16)
        lo = (a32 - hi.astype(F32)).astype(BF16)
        return jnp.stack([hi, lo])

    return dict(f1s=hilo(f1s), g2=hilo(g2), g2c=hilo(g2c), tw=jnp.asarray(twc, F32))


def _mm_const_lhs(c_ref, x, passes):
    xh = x.astype(BF16)
    out = _dot(c_ref[0], xh)
    if passes >= 2:
        out = out + _dot(c_ref[0], (x - xh.astype(F32)).astype(BF16))
    if passes >= 3:
        out = out + _dot(c_ref[1], xh)
    return out


def _mm_const_rhs(x, c_ref, passes):
    xh = x.astype(BF16)
    out = _dot(xh, c_ref[0])
    if passes >= 2:
        out = out + _dot((x - xh.astype(F32)).astype(BF16), c_ref[0])
    if passes >= 3:
        out = out + _dot(xh, c_ref[1])
    return out


def _kfft_kernel(x_ref, f1s_ref, tw_ref, g2_ref, o_ref, *, n1, cb, scale, passes):
    tr = tw_ref[:, 0:LANE]
    ti = tw_ref[:, LANE:2 * LANE]

    def body(cidx, carry):
        x = x_ref[cidx]
        p = _mm_const_lhs(f1s_ref, x, passes)
        ar = p[0:n1]
        ai = p[n1:2 * n1]
        br = ar * tr - ai * ti
        bi = ar * ti + ai * tr
        o_ref[cidx] = _mm_const_rhs(jnp.concatenate([br, bi], axis=1), g2_ref, passes) * scale
        return carry

    lax.fori_loop(0, cb, body, 0, unroll=2)


def _kernel_spectrum(kern_t, consts, *, cb, passes):
    w, n1, _ = kern_t.shape
    full3 = lambda a: pl.BlockSpec(a.shape, lambda i: (0, 0, 0))
    return pl.pallas_call(
        functools.partial(_kfft_kernel, n1=n1, cb=cb, scale=1.0 / (n1 * LANE), passes=passes),
        grid=(w // cb,),
        in_specs=[pl.BlockSpec((cb, n1, LANE), lambda i: (i, 0, 0)),
                  full3(consts["f1s"]),
                  pl.BlockSpec(consts["tw"].shape, lambda i: (0, 0)),
                  full3(consts["g2"])],
        out_specs=pl.BlockSpec((cb, n1, 2 * LANE), lambda i: (i, 0, 0)),
        out_shape=jax.ShapeDtypeStruct((w, n1, 2 * LANE), F32),
        compiler_params=_cparams(("parallel",)),
        name="hyena_filter_fft",
    )(kern_t, consts["f1s"], consts["tw"], consts["g2"])


def _fftconv_kernel(u_ref, kf_ref, f1a_ref, f1b_ref, tw_ref, g2_ref, g2c_ref, o_ref, *, n1, cb, passes):
    n1h = n1 // 2
    tr = tw_ref[:, 0:LANE]
    ti = tw_ref[:, LANE:2 * LANE]

    def body(cidx, carry):
        z = jnp.concatenate([u_ref[0, cidx], u_ref[1, cidx]], axis=1)
        p = _mm_const_lhs(f1a_ref, z, passes)
        ar = p[0:n1, 0:LANE] - p[n1:, LANE:]
        ai = p[0:n1, LANE:] + p[n1:, 0:LANE]
        a2 = jnp.concatenate([ar * tr - ai * ti, ar * ti + ai * tr], axis=1)
        x = _mm_const_rhs(a2, g2_ref, passes)
        xr = x[:, 0:LANE]
        xi = x[:, LANE:]
        kf = kf_ref[cidx]
        kr = kf[:, 0:LANE]
        ki = kf[:, LANE:]
        y = jnp.concatenate([xr * kr - xi * ki, xr * ki + xi * kr], axis=1)
        bm = _mm_const_rhs(y, g2c_ref, passes)
        br = bm[:, 0:LANE]
        bi = bm[:, LANE:]
        b2 = jnp.concatenate([br * tr + bi * ti, bi * tr - br * ti], axis=1)
        p2 = _mm_const_lhs(f1b_ref, b2, passes)
        o_ref[0, cidx] = p2[0:n1h, 0:LANE] + p2[n1h:, LANE:]
        o_ref[1, cidx] = p2[0:n1h, LANE:] - p2[n1h:, 0:LANE]
        return carry

    lax.fori_loop(0, cb, body, 0, unroll=4)


def _fftconv(u_t, kf, consts, *, cb, passes):
    bsz, w, n1h, _ = u_t.shape
    n1 = 2 * n1h
    f1s = consts["f1s"]
    f1a = f1s[:, :, :n1h]
    f1b = jnp.concatenate([f1s[:, 0:n1h, :], f1s[:, n1:n1 + n1h, :]], axis=1)
    full3 = lambda a: pl.BlockSpec(a.shape, lambda i, j: (0, 0, 0))
    return pl.pallas_call(
        functools.partial(_fftconv_kernel, n1=n1, cb=cb, passes=passes),
        grid=(bsz // 2, w // cb),
        in_specs=[pl.BlockSpec((2, cb, n1h, LANE), lambda i, j: (i, j, 0, 0)),
                  pl.BlockSpec((cb, n1, 2 * LANE), lambda i, j: (j, 0, 0)),
                  full3(f1a), full3(f1b),
                  pl.BlockSpec(consts["tw"].shape, lambda i, j: (0, 0)),
                  full3(consts["g2"]), full3(consts["g2c"])],
        out_specs=pl.BlockSpec((2, cb, n1h, LANE), lambda i, j: (i, j, 0, 0)),
        out_shape=jax.ShapeDtypeStruct(u_t.shape, F32),
        compiler_params=_cparams(("parallel", "parallel")),
        name="hyena_fftconv",
    )(u_t, kf, f1a, f1b, consts["tw"], consts["g2"], consts["g2c"])


def _merge_kernel(oh_ref, yc_ref, u_ref, x0_ref, ga_ref, gb_ref, x_ref, hb_ref, wa_ref, wb_ref, wo_ref, o_ref):
    ya = _dot(oh_ref[...], wa_ref[...])
    ybp = x0_ref[...] * (yc_ref[...] + u_ref[...] * hb_ref[...])
    yb = _dot(ybp.astype(BF16), wb_ref[...])
    merged = jax.nn.sigmoid(ga_ref[...]) * ya + jax.nn.sigmoid(gb_ref[...]) * yb
    o_ref[...] = x_ref[...] + _dot(merged.astype(BF16), wo_ref[...])


def _merge(oh, yc, u, x0c, proj2, x2, hbias, wa, wb, wo, *, gate_col0, tm):
    n, d = x2.shape
    gc = gate_col0 // d
    row = lambda c=0: pl.BlockSpec((tm, d), lambda i, c=c: (i, c))
    wfull = pl.BlockSpec((d, d), lambda i: (0, 0))
    return pl.pallas_call(
        _merge_kernel,
        grid=(n // tm,),
        in_specs=[row(), row(), row(), row(), row(gc), row(gc + 1), row(),
                  pl.BlockSpec((1, d), lambda i: (0, 0)), wfull, wfull, wfull],
        out_specs=row(),
        out_shape=jax.ShapeDtypeStruct((n, d), F32),
        compiler_params=_cparams(("parallel",)),
        name="branch_merge",
    )(oh, yc, u, x0c, proj2, proj2, x2, hbias.reshape(1, d), wa, wb, wo)


def _extract_topk(s, k, vals_ref, idx_ref, ids=None):
    nrow = s.shape[0]
    rid = lax.broadcasted_iota(I32, s.shape, 0)
    for j in range(k):
        m = jnp.max(s, axis=0, keepdims=True)
        ix = jnp.min(jnp.where(s == m, rid, nrow), axis=0, keepdims=True)
        hit = rid == ix
        vals_ref[j:j + 1, :] = m
        if ids is None:
            idx_ref[j:j + 1, :] = ix
        else:
            idx_ref[j:j + 1, :] = jnp.max(jnp.where(hit, ids, -1), axis=0, keepdims=True)
        s = jnp.where(hit, -jnp.inf, s)


def _peer_topk_kernel(q_ref, sk_ref, e_ref, g_ref, par_ref, s1_ref, i1_ref, s2_ref, i2_ref, ts_ref, te_ref):
    k = PEER_TOPK
    nk = PEER_NKEYS
    q = q_ref[...]
    s_a = _dot_nt(sk_ref[0, 0], q[:, 0:nk].astype(BF16))
    s_b = _dot_nt(sk_ref[0, 1], q[:, nk:2 * nk].astype(BF16))
    _extract_topk(s_a, k, s1_ref, i1_ref)
    _extract_topk(s_b, k, s2_ref, i2_ref)
    s1 = s1_ref[...]
    i1 = i1_ref[...]
    s2 = s2_ref[...]
    i2 = i2_ref[...]
    rid8 = lax.broadcasted_iota(I32, (SUBLANE, s1.shape[1]), 0)
    cs = [s1[0:1] + s2]
    ci = [i1[0:1] * nk + i2]
    for a in range(1, SUBLANE):
        nb = k // (a + 1)
        cs.append(jnp.where(rid8 < nb, s1[a:a + 1] + s2[0:SUBLANE], -jnp.inf))
        ci.append(i1[a:a + 1] * nk + i2[0:SUBLANE])
    cs.append(s1[SUBLANE:k] + s2[0:1])
    ci.append(i1[SUBLANE:k] * nk + i2[0:1])
    _extract_topk(jnp.concatenate(cs, axis=0), k, ts_ref, te_ref, ids=jnp.concatenate(ci, axis=0))
    ts = ts_ref[...]
    ex = jnp.exp(ts - ts[0:1])
    g_ref[0] = ex / jnp.sum(ex, axis=0, keepdims=True)
    te = te_ref[...]
    e_ref[0] = te >> 1
    par_ref[0] = (te & 1).astype(F32)


def _peer_topk(qp, subkeys_bf16, *, tt):
    n = qp.shape[0]
    h = PEER_HEADS
    k = PEER_TOPK
    nk = PEER_NKEYS
    ospec = pl.BlockSpec((1, k, tt), lambda i, hh: (hh, 0, i))
    return pl.pallas_call(
        _peer_topk_kernel,
        grid=(n // tt, h),
        in_specs=[pl.BlockSpec((tt, 2 * nk), lambda i, hh: (i, hh)),
                  pl.BlockSpec((1, 2, nk, nk), lambda i, hh: (hh, 0, 0, 0))],
        out_specs=[ospec, ospec, ospec],
        out_shape=[jax.ShapeDtypeStruct((h, k, n), I32), jax.ShapeDtypeStruct((h, k, n), F32),
                   jax.ShapeDtypeStruct((h, k, n), F32)],
        scratch_shapes=[pltpu.VMEM((k, tt), F32), pltpu.VMEM((k, tt), I32),
                        pltpu.VMEM((k, tt), F32), pltpu.VMEM((k, tt), I32),
                        pltpu.VMEM((k, tt), F32), pltpu.VMEM((k, tt), I32)],
        compiler_params=_cparams(("parallel", "arbitrary")),
        name="peer_topk",
    )(qp, subkeys_bf16)


def _row_as_tile(ref, t):
    row = ref[pl.ds(t, 1), :]
    return jnp.concatenate([row[:, k * LANE:(k + 1) * LANE] for k in range(SUBLANE)], axis=0)


def _lane_to_rows(row):
    return jnp.broadcast_to(row, (LANE, LANE)).T


def _sublane_sums8(ps, sub):
    lo4 = sub < 4
    c = []
    for i in range(4):
        a = jnp.where(lo4, ps[i], ps[i + 4])
        b = jnp.where(lo4, ps[i + 4], ps[i])
        c.append(a + pltpu.roll(b, 4, 0))
    m2 = (sub & 3) < 2
    dd = []
    for i in range(2):
        x = jnp.where(m2, c[i], pltpu.roll(c[i + 2], 2, 0))
        y = jnp.where(m2, pltpu.roll(c[i], 6, 0), c[i + 2])
        dd.append(x + y)
    m1 = (sub & 1) < 1
    x = jnp.where(m1, dd[0], pltpu.roll(dd[1], 1, 0))
    y = jnp.where(m1, pltpu.roll(dd[0], 7, 0), dd[1])
    return x + y


PEER_U_TOKENS_PER_ITER = 16


def _pack_expert_pairs(tab):
    ne, d = tab.shape
    bits = lax.bitcast_convert_type(tab.astype(BF16), jnp.uint16).astype(jnp.uint32).reshape(ne // 2, 2, d)
    word = (bits[:, 0] << 16) | bits[:, 1]
    return lax.bitcast_convert_type(word, I32).reshape(ne // 2, SUBLANE, LANE)


def _peer_u_kernel(idx_ref, x_ref, g_ref, par_ref, tab_ref, alo_ref, ahi_ref, rall_ref, sh_ref, *, tb, npair):
    sub = lax.broadcasted_iota(I32, (SUBLANE, LANE), 0)
    upper = jnp.int32(-65536)
    nu = PEER_U_TOKENS_PER_ITER

    def one_token(t, u):
        x = _row_as_tile(x_ref, t)
        par = par_ref[pl.ds(t, 1), :]
        sh_ref[u] = (16.0 * _lane_to_rows(par)).astype(I32)
        for gi in range(npair // SUBLANE):
            ps = []
            for i in range(SUBLANE):
                j = gi * SUBLANE + i
                w = tab_ref[idx_ref[t, j]]
                sh = jnp.broadcast_to(sh_ref[u, j:j + 1, :], (SUBLANE, LANE))
                ps.append(x * lax.bitcast_convert_type(lax.shift_left(w, sh) & upper, F32))
            rall_ref[u, gi * SUBLANE:(gi + 1) * SUBLANE, :] = _sublane_sums8(ps, sub)
        r = rall_ref[u]
        s = jnp.sum(r.T, axis=0, keepdims=True)
        a = jax.nn.gelu(s, approximate=True) * g_ref[pl.ds(t, 1), :]
        ahi = a * par
        ahi_ref[pl.ds(t, 1), :] = ahi
        alo_ref[pl.ds(t, 1), :] = a - ahi

    def tokens(i, carry):
        for u in range(nu):
            one_token(i * nu + u, u)
        return carry

    lax.fori_loop(0, tb // nu, tokens, 0)


def _peer_u(idx, x, g, par, tab, *, tb):
    n, npair = idx.shape
    assert npair == LANE and tb % PEER_U_TOKENS_PER_ITER == 0 and x.shape[1] == SUBLANE * LANE
    vrow = lambda: pl.BlockSpec((tb, npair), lambda i: (i, 0))
    return pl.pallas_call(
        functools.partial(_peer_u_kernel, tb=tb, npair=npair),
        grid=(n // tb,),
        in_specs=[pl.BlockSpec((tb, npair), lambda i: (i, 0), memory_space=pltpu.SMEM),
                  pl.BlockSpec((tb, SUBLANE * LANE), lambda i: (i, 0)),
                  vrow(), vrow(),
                  pl.BlockSpec(tab.shape, lambda i: (0, 0, 0), pipeline_mode=pl.Buffered(1))],
        out_specs=[vrow(), vrow()],
        out_shape=[jax.ShapeDtypeStruct((n, npair), F32)] * 2,
        scratch_shapes=[pltpu.VMEM((PEER_U_TOKENS_PER_ITER, npair, LANE), F32),
                        pltpu.VMEM((PEER_U_TOKENS_PER_ITER, npair, LANE), I32)],
        compiler_params=_cparams(("parallel",)),
        name="peer_expert_in",
    )(idx, x, g, par, tab)


def _peer_v_consts(npair):
    rows = 2 * SUBLANE
    col = np.arange(npair * rows)
    expand = (col[None, :] // rows) == np.arange(npair)[:, None]
    s = np.arange(SUBLANE)[:, None]
    plo = (col[None, :] % rows) == s
    phi = (col[None, :] % rows) == s + SUBLANE
    return jnp.asarray(expand, BF16), jnp.asarray(plo, F32), jnp.asarray(phi, F32)


def _peer_v_kernel(idx_ref, alo_ref, ahi_ref, h_ref, gf_ref, ex_ref, plo_ref, phi_ref, tab_ref, o_ref,
                   elo_ref, ehi_ref, *, tb, npair):
    half = npair // 2
    kh = half * 2 * SUBLANE

    def lane_expand(a):
        ah = a.astype(BF16)
        al = (a - ah.astype(F32)).astype(BF16)
        return _dot(ah, ex_ref[...]) + _dot(al, ex_ref[...])

    elo_ref[...] = lane_expand(alo_ref[...])
    ehi_ref[...] = lane_expand(ahi_ref[...])

    def token(t, carry):
        tiles = [tab_ref[idx_ref[t, j]] for j in range(npair)]
        rhs = jnp.concatenate([jnp.concatenate(tiles[:half], axis=0),
                               jnp.concatenate(tiles[half:], axis=0)], axis=1)
        coef = elo_ref[pl.ds(t, 1), :] * plo_ref[...] + ehi_ref[pl.ds(t, 1), :] * phi_ref[...]
        c2 = jnp.concatenate([coef[:, :kh], coef[:, kh:]], axis=0)
        ch = c2.astype(BF16)
        cl = (c2 - ch.astype(F32)).astype(BF16)
        r = _dot(jnp.concatenate([ch, cl], axis=0), rhs)
        acc = ((r[0:SUBLANE, 0:LANE] + r[SUBLANE:2 * SUBLANE, LANE:])
               + (r[2 * SUBLANE:3 * SUBLANE, 0:LANE] + r[3 * SUBLANE:, LANE:]))
        hh = _row_as_tile(h_ref, t) + acc
        ss = jnp.sum(jnp.sum(hh * hh, axis=1, keepdims=True), axis=0, keepdims=True)
        res = hh * lax.rsqrt(ss * (1.0 / (SUBLANE * LANE)) + RMS_EPS) * gf_ref[...]
        o_ref[pl.ds(t, 1), :] = jnp.concatenate([res[k:k + 1, :] for k in range(SUBLANE)], axis=1)
        return carry

    lax.fori_loop(0, tb, token, 0, unroll=16)


def _peer_v(idx, alo, ahi, h, gfin, tab, *, tb):
    n, npair = idx.shape
    assert npair == LANE and h.shape[1] == SUBLANE * LANE
    ex, plo, phi = _peer_v_consts(npair)
    vrow = lambda: pl.BlockSpec((tb, npair), lambda i: (i, 0))
    full = lambda a: pl.BlockSpec(a.shape, lambda i: (0, 0))
    return pl.pallas_call(
        functools.partial(_peer_v_kernel, tb=tb, npair=npair),
        grid=(n // tb,),
        in_specs=[pl.BlockSpec((tb, npair), lambda i: (i, 0), memory_space=pltpu.SMEM),
                  vrow(), vrow(),
                  pl.BlockSpec((tb, SUBLANE * LANE), lambda i: (i, 0)),
                  pl.BlockSpec((SUBLANE, LANE), lambda i: (0, 0)),
                  full(ex), full(plo), full(phi),
                  pl.BlockSpec(tab.shape, lambda i: (0, 0, 0), pipeline_mode=pl.Buffered(1))],
        out_specs=pl.BlockSpec((tb, SUBLANE * LANE), lambda i: (i, 0)),
        out_shape=jax.ShapeDtypeStruct(h.shape, F32),
        scratch_shapes=[pltpu.VMEM((tb, 2 * SUBLANE * npair), F32), pltpu.VMEM((tb, 2 * SUBLANE * npair), F32)],
        compiler_params=_cparams(("parallel",)),
        name="peer_expert_out",
    )(idx, alo, ahi, h, gfin, ex, plo, phi, tab)


def _pick(n, pref):
    t = min(n, pref)
    while n % t:
        t //= 2
    return t


def _layer(h, lb, p, fft_passes):
    bsz, seq, d = h.shape
    n = bsz * seq
    kw = HGRN_HEADS * HGRN_DK
    hyena_col0 = 3 * kw + 2 * d
    gate_col0 = hyena_col0 + 3 * d

    proj2 = _norm_matmul(h.reshape(n, d), p["norm_mix_g"], p["w_in"].astype(BF16),
                         tm=_pick(n, 1024), tn=1024)
    proj = proj2.reshape(bsz, seq, -1)

    oh = _hgrn(proj, lb, p["hgrn_norm_g"], tg=_pick(seq, 1024))

    u, x0c = _shortconv(proj, p["hyena_conv_w"], p["hyena_conv_b"], col0=hyena_col0, width=d,
                        tc=_pick(seq, 1024))
    kern = _hyena_conv_kernel(seq, p["filt_w1"], p["filt_b1"], p["filt_freq1"], p["filt_w2"], p["filt_b2"],
                              p["filt_freq2"], p["filt_w3"], p["filt_decay"], tl=_pick(seq, 512))
    n1 = 2 * seq // LANE
    consts = _fft_consts(n1)
    kf = _kernel_spectrum(kern.T.reshape(d, n1, LANE), consts, cb=_pick(d, 8), passes=3)
    u_t = jnp.swapaxes(u, 1, 2).reshape(bsz, d, n1 // 2, LANE)
    y_t = _fftconv(u_t, kf, consts, cb=_pick(d, 16), passes=fft_passes)
    yc = jnp.swapaxes(y_t.reshape(bsz, d, seq), 1, 2)

    h1 = _merge(oh.reshape(n, d), yc.reshape(n, d), u.reshape(n, d), x0c.reshape(n, d), proj2,
                h.reshape(n, d), p["hyena_bias"], p["w_branch_a"].astype(BF16),
                p["w_branch_b"].astype(BF16), p["w_out"].astype(BF16), gate_col0=gate_col0, tm=_pick(n, 256))

    qp, hn = _norm_matmul(h1, p["norm_ffn_g"], p["peer_w_q"].astype(BF16), tm=_pick(n, 1024),
                          tn=_pick(p["peer_w_q"].shape[1], 1024), emit_xn=True)
    tiles, gates, halves = _peer_topk(qp, p["peer_subkeys"].astype(BF16), tt=_pick(n, 256))
    npair = PEER_HEADS * PEER_TOPK
    tokmajor = lambda a: a.transpose(2, 0, 1).reshape(n, npair)
    idx = tokmajor(tiles)
    ne = p["peer_u"].shape[0]
    utab = _pack_expert_pairs(p["peer_u"])
    vtab = p["peer_v"].astype(BF16).reshape(ne // 2, 2 * SUBLANE, LANE)
    tb = _pick(n, 128)
    alo, ahi = _peer_u(idx, hn, tokmajor(gates), tokmajor(halves), utab, tb=tb)
    return h1, idx, alo, ahi, vtab, tb


def kernel(x, norm_mix_g, w_in, hgrn_lb_logits, hgrn_norm_g, hyena_conv_w, hyena_conv_b, filt_w1, filt_b1, filt_freq1, filt_w2, filt_b2, filt_freq2, filt_w3, filt_decay, hyena_bias, w_branch_a, w_branch_b, w_out, norm_ffn_g, peer_w_q, peer_subkeys, peer_u, peer_v, norm_final_g):
    bsz, seq, d = x.shape
    n = bsz * seq
    depth = w_in.shape[0]
    assert depth == 1, "the fused final norm assumes a single layer"
    lb_table = jnp.cumsum(jax.nn.softmax(hgrn_lb_logits.astype(F32), axis=0), axis=0)
    p = dict(norm_mix_g=norm_mix_g[0], w_in=w_in[0], hgrn_norm_g=hgrn_norm_g[0], hyena_conv_w=hyena_conv_w[0],
             hyena_conv_b=hyena_conv_b[0], filt_w1=filt_w1[0], filt_b1=filt_b1[0], filt_freq1=filt_freq1[0],
             filt_w2=filt_w2[0], filt_b2=filt_b2[0], filt_freq2=filt_freq2[0], filt_w3=filt_w3[0],
             filt_decay=filt_decay[0], hyena_bias=hyena_bias[0], w_branch_a=w_branch_a[0],
             w_branch_b=w_branch_b[0], w_out=w_out[0], norm_ffn_g=norm_ffn_g[0], peer_w_q=peer_w_q[0],
             peer_subkeys=peer_subkeys[0], peer_u=peer_u[0], peer_v=peer_v[0])
    h1, idx, alo, ahi, vtab, tb = _layer(x, lb_table[0], p, fft_passes=1)
    out = _peer_v(idx, alo, ahi, h1, norm_final_g.reshape(SUBLANE, LANE), vtab, tb=tb)
    return out.reshape(bsz, seq, d)
```

```python
import functools
import math

import numpy as np
import jax
import jax.numpy as jnp
from jax import lax
from jax.experimental import pallas as pl
from jax.experimental.pallas import tpu as pltpu

F32 = jnp.float32
BF16 = jnp.bfloat16
I32 = jnp.int32

RMS_EPS = 1e-6
LANE = 128
SUBLANE = 8
VMEM_LIMIT = 56 * 1024 * 1024

HGRN_HEADS = 8
HGRN_DK = 128
HGRN_CHUNK = 128
HYENA_BANDS = 16
HYENA_HIDDEN = 64
PEER_HEADS = 8
PEER_NKEYS = 128
PEER_TOPK = 16

HIGHEST = lax.Precision.HIGHEST


def _cparams(sem, vmem=VMEM_LIMIT):
    return pltpu.CompilerParams(dimension_semantics=sem, vmem_limit_bytes=vmem)


def _dot(a, b):
    return jnp.dot(a, b, preferred_element_type=F32)


def _dot_nt(a, b):
    return lax.dot_general(a, b, (((1,), (1,)), ((), ())), preferred_element_type=F32)


def _normmm_kernel(x_ref, g_ref, w_ref, o_ref, *rest, emit_xn):
    if emit_xn:
        xn_out_ref, xn_ref = rest
    else:
        (xn_ref,) = rest

    @pl.when(pl.program_id(1) == 0)
    def _():
        x = x_ref[...]
        ms = jnp.mean(x * x, axis=-1, keepdims=True)
        xn = x * lax.rsqrt(ms + RMS_EPS) * g_ref[...]
        xn_ref[...] = xn.astype(BF16)
        if emit_xn:
            xn_out_ref[...] = xn

    o_ref[...] = _dot(xn_ref[...], w_ref[...])


def _norm_matmul(x, g, w_bf16, *, tm, tn, emit_xn=False):
    n, d = x.shape
    nout = w_bf16.shape[1]
    out_shape = [jax.ShapeDtypeStruct((n, nout), F32)]
    out_specs = [pl.BlockSpec((tm, tn), lambda i, j: (i, j))]
    if emit_xn:
        out_shape.append(jax.ShapeDtypeStruct((n, d), F32))
        out_specs.append(pl.BlockSpec((tm, d), lambda i, j: (i, 0)))
    res = pl.pallas_call(
        functools.partial(_normmm_kernel, emit_xn=emit_xn),
        grid=(n // tm, nout // tn),
        in_specs=[pl.BlockSpec((tm, d), lambda i, j: (i, 0)),
                  pl.BlockSpec((1, d), lambda i, j: (0, 0)),
                  pl.BlockSpec((d, tn), lambda i, j: (0, j))],
        out_specs=out_specs,
        out_shape=out_shape,
        scratch_shapes=[pltpu.VMEM((tm, d), BF16)],
        compiler_params=_cparams(("parallel", "arbitrary")),
        name="norm_matmul",
    )(x, g.reshape(1, d), w_bf16)
    return res if emit_xn else res[0]


def _hgrn_consts(c):
    nlev = int(math.log2(c))
    t = np.arange(c)
    m = np.zeros((2, (nlev + 2) * c, c), np.float32)
    up = np.zeros((2, nlev, c, LANE), np.float32)
    mask = np.zeros((nlev + 1, c, c), np.float32)
    for d in range(2):
        p = t if d == 0 else c - 1 - t
        m[d, 0:c] = p[None, :] <= p[:, None]
        for l in range(nlev):
            hs = 1 << l
            pmid = (p // (2 * hs)) * (2 * hs) + hs - 1
            m[d, (l + 1) * c:(l + 2) * c] = p[None, :] <= pmid[:, None]
            up[d, l] = (((p // hs) % 2) == 1)[:, None]
        m[d, (nlev + 1) * c:] = 1.0
    mask[0] = np.eye(c)
    for l in range(nlev):
        hs = 1 << l
        mask[l + 1] = (t[:, None] // (2 * hs)) == (t[None, :] // (2 * hs))
    m3 = np.concatenate([m, m, m], axis=2)
    return jnp.asarray(m3, BF16), jnp.asarray(up, F32), jnp.asarray(mask, F32), nlev


def _split3(x):
    hi = x.astype(BF16)
    r1 = x - hi.astype(F32)
    mid = r1.astype(BF16)
    lo = (r1 - mid.astype(F32)).astype(BF16)
    return hi, mid, lo


def _hgrn_kernel(q_ref, z_ref, v_ref, og_ref, lb_ref, gn_ref, m_ref, up_ref, mask_ref,
                 o_ref, st_ref, of_ref, *, c, nc, nlev, hpb):
    d = pl.program_id(2)
    g = pl.program_id(3)
    ng = pl.num_programs(3)
    tg = nc * c

    @pl.when(g == 0)
    def _():
        st_ref[...] = jnp.zeros_like(st_ref)

    grp = jnp.where(d == 0, g, ng - 1 - g)

    def head_chunk(hh, r0):
        sl = slice(hh * LANE, (hh + 1) * LANE)
        lb = lb_ref[:, sl]
        qraw = q_ref[0, pl.ds(r0, c), sl]
        z = z_ref[0, pl.ds(r0, c), sl]
        v = v_ref[0, pl.ds(r0, c), sl]
        logf = jnp.log(lb + (1.0 - lb) * jax.nn.sigmoid(z))
        kk = (1.0 - lb) * jax.nn.sigmoid(-z)
        q = qraw * jax.nn.sigmoid(qraw)

        l3 = jnp.concatenate(_split3(logf), axis=0)
        r = _dot(m_ref[0], l3)
        b = r[0:c]
        blast = r[(nlev + 1) * c:(nlev + 1) * c + 1]

        qb = q.astype(BF16)
        kb = kk.astype(BF16)
        sc = mask_ref[0] * _dot_nt(qb, kb)
        for l in range(nlev):
            rl = r[(l + 1) * c:(l + 2) * c]
            up = up_ref[0, l]
            e = jnp.exp((b - rl) * (2.0 * up - 1.0))
            eu = e * up
            qu = (q * eu).astype(BF16)
            kl = (kk * (e - eu)).astype(BF16)
            sc = sc + mask_ref[l + 1] * _dot_nt(qu, kl)

        vb = v.astype(BF16)
        st = st_ref[hh]
        inter = _dot_nt((q * jnp.exp(b)).astype(BF16), st.astype(BF16))
        o = inter + _dot(sc.astype(BF16), vb)
        kd = (kk * jnp.exp(jnp.minimum(blast - b, 0.0))).astype(BF16)
        st_ref[hh] = st * jnp.exp(blast) + _dot(v.T.astype(BF16), kd)
        return o

    def chunk(i, carry):
        ci = jnp.where(d == 0, i, nc - 1 - i)
        r0 = pl.multiple_of(ci * c, c)
        outs = [head_chunk(hh, r0) for hh in range(hpb)]
        row = pl.multiple_of(grp * tg + r0, c)

        @pl.when(d == 0)
        def _():
            for hh in range(hpb):
                of_ref[pl.ds(row, c), hh * LANE:(hh + 1) * LANE] = outs[hh]

        @pl.when(d == 1)
        def _():
            for hh in range(hpb):
                sl = slice(hh * LANE, (hh + 1) * LANE)
                tot = of_ref[pl.ds(row, c), sl] + outs[hh]
                ms = jnp.mean(tot * tot, axis=-1, keepdims=True)
                on = tot * lax.rsqrt(ms + RMS_EPS) * gn_ref[...]
                og = og_ref[0, pl.ds(r0, c), sl]
                o_ref[0, pl.ds(r0, c), sl] = (on * (og * jax.nn.sigmoid(og))).astype(BF16)

        return carry

    lax.fori_loop(0, nc, chunk, 0)


def _hgrn(proj, lb, gn, *, tg, hpb=4):
    bsz, seq, _ = proj.shape
    c = HGRN_CHUNK
    nc = tg // c
    ng = seq // tg
    h = HGRN_HEADS // hpb
    bw = hpb * LANE
    m, up, mask, nlev = _hgrn_consts(c)

    def rows(d, g):
        return jnp.where(d == 0, g, ng - 1 - g)

    def outrows(d, g):
        return jnp.where(d == 0, ng - 1, ng - 1 - g)

    in_specs = [
        pl.BlockSpec((1, tg, bw), lambda b, hh, d, g: (b, rows(d, g), hh)),
        pl.BlockSpec((1, tg, bw), lambda b, hh, d, g: (b, rows(d, g), h * (1 + d) + hh)),
        pl.BlockSpec((1, tg, bw), lambda b, hh, d, g: (b, rows(d, g), 3 * h + hh)),
        pl.BlockSpec((1, tg, bw), lambda b, hh, d, g: (b, outrows(d, g), 4 * h + hh)),
        pl.BlockSpec((1, bw), lambda b, hh, d, g: (0, hh)),
        pl.BlockSpec((1, LANE), lambda b, hh, d, g: (0, 0)),
        pl.BlockSpec((1,) + m.shape[1:], lambda b, hh, d, g: (d, 0, 0)),
        pl.BlockSpec((1,) + up.shape[1:], lambda b, hh, d, g: (d, 0, 0, 0)),
        pl.BlockSpec(mask.shape, lambda b, hh, d, g: (0, 0, 0)),
    ]
    return pl.pallas_call(
        functools.partial(_hgrn_kernel, c=c, nc=nc, nlev=nlev, hpb=hpb),
        grid=(bsz, h, 2, ng),
        in_specs=in_specs,
        out_specs=pl.BlockSpec((1, tg, bw), lambda b, hh, d, g: (b, outrows(d, g), hh)),
        out_shape=jax.ShapeDtypeStruct((bsz, seq, h * bw), BF16),
        scratch_shapes=[pltpu.VMEM((hpb, LANE, HGRN_DK), F32), pltpu.VMEM((seq, bw), F32)],
        compiler_params=_cparams(("parallel", "parallel", "arbitrary", "arbitrary")),
        name="hgrn2_scan",
    )(proj, proj, proj, proj, lb.reshape(1, -1), gn.reshape(1, -1), m, up, mask)


def _shortconv_kernel(*refs, tc):
    (x0_ref, x0p_ref, x0n_ref, x1_ref, x1p_ref, x1n_ref, x2_ref, x2p_ref, x2n_ref,
     w0_ref, w1_ref, w2_ref, b0_ref, b1_ref, b2_ref, u_ref, x0c_ref) = refs
    i = pl.program_id(2)
    n = pl.num_programs(2)
    has_prev = (i > 0).astype(F32)
    has_next = (i < n - 1).astype(F32)
    rid = lax.broadcasted_iota(I32, (tc, LANE), 0)

    def conv(x_ref, xp_ref, xn_ref, w_ref, b_ref):
        x = x_ref[0]
        prev_row = xp_ref[0, SUBLANE - 1:SUBLANE, :] * has_prev
        next_row = xn_ref[0, 0:1, :] * has_next
        xm1 = jnp.where(rid == 0, prev_row, pltpu.roll(x, 1, 0))
        xp1 = jnp.where(rid == tc - 1, next_row, pltpu.roll(x, tc - 1, 0))
        w = w_ref[...]
        return ((b_ref[...] + xm1 * w[0:1]) + x * w[1:2]) + xp1 * w[2:3]

    x0c = conv(x0_ref, x0p_ref, x0n_ref, w0_ref, b0_ref)
    x1c = conv(x1_ref, x1p_ref, x1n_ref, w1_ref, b1_ref)
    vc = conv(x2_ref, x2p_ref, x2n_ref, w2_ref, b2_ref)
    x0c_ref[0] = x0c
    u_ref[0] = vc * x1c


def _shortconv(proj, conv_w, conv_b, *, col0, width, tc):
    bsz, seq, _ = proj.shape
    ncb = width // LANE
    cb0 = col0 // LANE
    nt = seq // tc
    rb = tc // SUBLANE
    nrb = seq // SUBLANE

    def stream(k):
        off = cb0 + k * ncb
        return [
            pl.BlockSpec((1, tc, LANE), lambda b, cc, i, off=off: (b, i, off + cc)),
            pl.BlockSpec((1, SUBLANE, LANE), lambda b, cc, i, off=off: (b, jnp.maximum(i * rb - 1, 0), off + cc)),
            pl.BlockSpec((1, SUBLANE, LANE), lambda b, cc, i, off=off: (b, jnp.minimum((i + 1) * rb, nrb - 1), off + cc)),
        ]

    wspecs = [pl.BlockSpec((3, LANE), lambda b, cc, i, k=k: (0, k * ncb + cc)) for k in range(3)]
    bspecs = [pl.BlockSpec((1, LANE), lambda b, cc, i, k=k: (0, k * ncb + cc)) for k in range(3)]
    ospec = pl.BlockSpec((1, tc, LANE), lambda b, cc, i: (b, i, cc))
    cb2 = conv_b.reshape(1, -1)
    return pl.pallas_call(
        functools.partial(_shortconv_kernel, tc=tc),
        grid=(bsz, ncb, nt),
        in_specs=stream(0) + stream(1) + stream(2) + wspecs + bspecs,
        out_specs=[ospec, ospec],
        out_shape=[jax.ShapeDtypeStruct((bsz, seq, width), F32)] * 2,
        compiler_params=_cparams(("parallel", "parallel", "arbitrary")),
        name="hyena_shortconv",
    )(*([proj] * 9), conv_w, conv_w, conv_w, cb2, cb2, cb2)


def _filter_kernel(band_ref, w1_ref, b1_ref, f1_ref, w2_ref, b2_ref, f2_ref, w3_ref, dec_ref, o_ref,
                   *, tl, seq):
    i = pl.program_id(0)
    n = i * tl + lax.broadcasted_iota(I32, (tl, LANE), 0)
    pos = jnp.where(n < seq, n, 2 * seq - n).astype(F32)
    lane = lax.broadcasted_iota(I32, (tl, LANE), 1)
    t = pos / float(max(seq - 1, 1))
    ang = (2.0 * math.pi / seq) * pos * band_ref[...]
    z = jnp.where(lane == 0, t,
                  jnp.where(lane <= HYENA_BANDS, jnp.cos(ang),
                            jnp.where(lane <= 2 * HYENA_BANDS, -jnp.sin(ang), 0.0)))
    hid = jnp.sin(f1_ref[...] * (jnp.dot(z, w1_ref[...], precision=HIGHEST, preferred_element_type=F32) + b1_ref[...]))
    hid = jnp.sin(f2_ref[...] * (jnp.dot(hid, w2_ref[...], precision=HIGHEST, preferred_element_type=F32) + b2_ref[...]))
    filt = jnp.dot(hid, w3_ref[...], precision=HIGHEST, preferred_element_type=F32)
    filt = filt * jnp.exp(-t[:, 0:1] * jnp.abs(dec_ref[...]))
    o_ref[...] = jnp.where(n[:, 0:1] == seq, 0.0, filt)


def _hyena_conv_kernel(seq, w1, b1, f1, w2, b2, f2, w3, decay, *, tl):
    emb, hid = w1.shape
    wout = w3.shape[1] // 2
    nb = seq // tl
    bands = jnp.linspace(1e-4, HYENA_BANDS - 1, HYENA_BANDS, dtype=F32)
    band_row = jnp.zeros((1, LANE), F32).at[0, 1:1 + HYENA_BANDS].set(bands)
    band_row = band_row.at[0, 1 + HYENA_BANDS:1 + 2 * HYENA_BANDS].set(bands)

    def padm(a, r, c):
        return jnp.zeros((r, c), F32).at[:a.shape[0], :a.shape[1]].set(a.astype(F32))

    w1p = padm(w1, LANE, LANE)
    w2p = padm(w2, LANE, LANE)
    w3p = padm(w3, LANE, 2 * wout)
    vec = lambda a: padm(a.reshape(1, -1), 1, LANE)
    full = lambda shp: pl.BlockSpec(shp, lambda i: (0, 0))
    half = lambda rows: pl.BlockSpec((rows, wout), lambda i: (0, i // nb))
    return pl.pallas_call(
        functools.partial(_filter_kernel, tl=tl, seq=seq),
        grid=(2 * nb,),
        in_specs=[full((1, LANE)), full((LANE, LANE)), full((1, LANE)), full((1, LANE)),
                  full((LANE, LANE)), full((1, LANE)), full((1, LANE)), half(LANE), half(1)],
        out_specs=pl.BlockSpec((tl, wout), lambda i: (i, 0)),
        out_shape=jax.ShapeDtypeStruct((2 * seq, wout), F32),
        compiler_params=_cparams(("parallel",)),
        name="hyena_filter_mlp",
    )(band_row, w1p, vec(b1), vec(f1), w2p, vec(b2), vec(f2), w3p, decay.reshape(1, -1).astype(F32))


def _fft_consts(n1):
    n2 = LANE
    n = n1 * n2
    k1 = np.arange(n1)
    f1 = np.exp(-2j * np.pi * np.outer(k1, k1) / n1)
    k2 = np.arange(n2)
    f2 = np.exp(-2j * np.pi * np.outer(k2, k2) / n2)
    tw = np.exp(-2j * np.pi * np.outer(k1, k2) / n)
    f1s = np.concatenate([f1.real, f1.imag], axis=0)
    g2 = np.block([[f2.real, f2.imag], [-f2.imag, f2.real]])
    g2c = np.block([[f2.real, -f2.imag], [f2.imag, f2.real]])
    twc = np.concatenate([tw.real, tw.imag], axis=1)

    def hilo(a):
        a32 = jnp.asarray(a, F32)
        hi = a32.astype(BF16)
        lo = (a32 - hi.astype(F32)).astype(BF16)
        return jnp.stack([hi, lo])

    return dict(f1s=hilo(f1s), g2=hilo(g2), g2c=hilo(g2c), tw=jnp.asarray(twc, F32))


def _mm_const_lhs(c_ref, x, passes):
    xh = x.astype(BF16)
    out = _dot(c_ref[0], xh)
    if passes >= 2:
        out = out + _dot(c_ref[0], (x - xh.astype(F32)).astype(BF16))
    if passes >= 3:
        out = out + _dot(c_ref[1], xh)
    return out


def _mm_const_rhs(x, c_ref, passes):
    xh = x.astype(BF16)
    out = _dot(xh, c_ref[0])
    if passes >= 2:
        out = out + _dot((x - xh.astype(F32)).astype(BF16), c_ref[0])
    if passes >= 3:
        out = out + _dot(xh, c_ref[1])
    return out


def _kfft_kernel(x_ref, f1s_ref, tw_ref, g2_ref, o_ref, *, n1, cb, scale, passes):
    tr = tw_ref[:, 0:LANE]
    ti = tw_ref[:, LANE:2 * LANE]

    def body(cidx, carry):
        x = x_ref[cidx]
        p = _mm_const_lhs(f1s_ref, x, passes)
        ar = p[0:n1]
        ai = p[n1:2 * n1]
        br = ar * tr - ai * ti
        bi = ar * ti + ai * tr
        o_ref[cidx] = _mm_const_rhs(jnp.concatenate([br, bi], axis=1), g2_ref, passes) * scale
        return carry

    lax.fori_loop(0, cb, body, 0, unroll=2)


def _kernel_spectrum(kern_t, consts, *, cb, passes):
    w, n1, _ = kern_t.shape
    full3 = lambda a: pl.BlockSpec(a.shape, lambda i: (0, 0, 0))
    return pl.pallas_call(
        functools.partial(_kfft_kernel, n1=n1, cb=cb, scale=1.0 / (n1 * LANE), passes=passes),
        grid=(w // cb,),
        in_specs=[pl.BlockSpec((cb, n1, LANE), lambda i: (i, 0, 0)),
                  full3(consts["f1s"]),
                  pl.BlockSpec(consts["tw"].shape, lambda i: (0, 0)),
                  full3(consts["g2"])],
        out_specs=pl.BlockSpec((cb, n1, 2 * LANE), lambda i: (i, 0, 0)),
        out_shape=jax.ShapeDtypeStruct((w, n1, 2 * LANE), F32),
        compiler_params=_cparams(("parallel",)),
        name="hyena_filter_fft",
    )(kern_t, consts["f1s"], consts["tw"], consts["g2"])


def _fftconv_kernel(u_ref, kf_ref, f1a_ref, f1b_ref, tw_ref, g2_ref, g2c_ref, o_ref, *, n1, cb, passes):
    n1h = n1 // 2
    tr = tw_ref[:, 0:LANE]
    ti = tw_ref[:, LANE:2 * LANE]

    def body(cidx, carry):
        z = jnp.concatenate([u_ref[0, cidx], u_ref[1, cidx]], axis=1)
        p = _mm_const_lhs(f1a_ref, z, passes)
        ar = p[0:n1, 0:LANE] - p[n1:, LANE:]
        ai = p[0:n1, LANE:] + p[n1:, 0:LANE]
        a2 = jnp.concatenate([ar * tr - ai * ti, ar * ti + ai * tr], axis=1)
        x = _mm_const_rhs(a2, g2_ref, passes)
        xr = x[:, 0:LANE]
        xi = x[:, LANE:]
        kf = kf_ref[cidx]
        kr = kf[:, 0:LANE]
        ki = kf[:, LANE:]
        y = jnp.concatenate([xr * kr - xi * ki, xr * ki + xi * kr], axis=1)
        bm = _mm_const_rhs(y, g2c_ref, passes)
        br = bm[:, 0:LANE]
        bi = bm[:, LANE:]
        b2 = jnp.concatenate([br * tr + bi * ti, bi * tr - br * ti], axis=1)
        p2 = _mm_const_lhs(f1b_ref, b2, passes)
        o_ref[0, cidx] = p2[0:n1h, 0:LANE] + p2[n1h:, LANE:]
        o_ref[1, cidx] = p2[0:n1h, LANE:] - p2[n1h:, 0:LANE]
        return carry

    lax.fori_loop(0, cb, body, 0, unroll=4)


def _fftconv(u_t, kf, consts, *, cb, passes):
    bsz, w, n1h, _ = u_t.shape
    n1 = 2 * n1h
    f1s = consts["f1s"]
    f1a = f1s[:, :, :n1h]
    f1b = jnp.concatenate([f1s[:, 0:n1h, :], f1s[:, n1:n1 + n1h, :]], axis=1)
    full3 = lambda a: pl.BlockSpec(a.shape, lambda i, j: (0, 0, 0))
    return pl.pallas_call(
        functools.partial(_fftconv_kernel, n1=n1, cb=cb, passes=passes),
        grid=(bsz // 2, w // cb),
        in_specs=[pl.BlockSpec((2, cb, n1h, LANE), lambda i, j: (i, j, 0, 0)),
                  pl.BlockSpec((cb, n1, 2 * LANE), lambda i, j: (j, 0, 0)),
                  full3(f1a), full3(f1b),
                  pl.BlockSpec(consts["tw"].shape, lambda i, j: (0, 0)),
                  full3(consts["g2"]), full3(consts["g2c"])],
        out_specs=pl.BlockSpec((2, cb, n1h, LANE), lambda i, j: (i, j, 0, 0)),
        out_shape=jax.ShapeDtypeStruct(u_t.shape, F32),
        compiler_params=_cparams(("parallel", "parallel")),
        name="hyena_fftconv",
    )(u_t, kf, f1a, f1b, consts["tw"], consts["g2"], consts["g2c"])


def _merge_kernel(oh_ref, yc_ref, u_ref, x0_ref, ga_ref, gb_ref, x_ref, hb_ref, wa_ref, wb_ref, wo_ref, o_ref):
    ya = _dot(oh_ref[...], wa_ref[...])
    ybp = x0_ref[...] * (yc_ref[...] + u_ref[...] * hb_ref[...])
    yb = _dot(ybp.astype(BF16), wb_ref[...])
    merged = jax.nn.sigmoid(ga_ref[...]) * ya + jax.nn.sigmoid(gb_ref[...]) * yb
    o_ref[...] = x_ref[...] + _dot(merged.astype(BF16), wo_ref[...])


def _merge(oh, yc, u, x0c, proj2, x2, hbias, wa, wb, wo, *, gate_col0, tm):
    n, d = x2.shape
    gc = gate_col0 // d
    row = lambda c=0: pl.BlockSpec((tm, d), lambda i, c=c: (i, c))
    wfull = pl.BlockSpec((d, d), lambda i: (0, 0))
    return pl.pallas_call(
        _merge_kernel,
        grid=(n // tm,),
        in_specs=[row(), row(), row(), row(), row(gc), row(gc + 1), row(),
                  pl.BlockSpec((1, d), lambda i: (0, 0)), wfull, wfull, wfull],
        out_specs=row(),
        out_shape=jax.ShapeDtypeStruct((n, d), F32),
        compiler_params=_cparams(("parallel",)),
        name="branch_merge",
    )(oh, yc, u, x0c, proj2, proj2, x2, hbias.reshape(1, d), wa, wb, wo)


def _extract_topk(s, k, vals_ref, idx_ref, ids=None):
    nrow = s.shape[0]
    rid = lax.broadcasted_iota(I32, s.shape, 0)
    for j in range(k):
        m = jnp.max(s, axis=0, keepdims=True)
        ix = jnp.min(jnp.where(s == m, rid, nrow), axis=0, keepdims=True)
        hit = rid == ix
        vals_ref[j:j + 1, :] = m
        if ids is None:
            idx_ref[j:j + 1, :] = ix
        else:
            idx_ref[j:j + 1, :] = jnp.max(jnp.where(hit, ids, -1), axis=0, keepdims=True)
        s = jnp.where(hit, -jnp.inf, s)


def _peer_topk_kernel(q_ref, sk_ref, e_ref, g_ref, par_ref, s1_ref, i1_ref, s2_ref, i2_ref, ts_ref, te_ref):
    k = PEER_TOPK
    nk = PEER_NKEYS
    q = q_ref[...]
    s_a = _dot_nt(sk_ref[0, 0], q[:, 0:nk].astype(BF16))
    s_b = _dot_nt(sk_ref[0, 1], q[:, nk:2 * nk].astype(BF16))
    _extract_topk(s_a, k, s1_ref, i1_ref)
    _extract_topk(s_b, k, s2_ref, i2_ref)
    s1 = s1_ref[...]
    i1 = i1_ref[...]
    s2 = s2_ref[...]
    i2 = i2_ref[...]
    rid8 = lax.broadcasted_iota(I32, (SUBLANE, s1.shape[1]), 0)
    cs = [s1[0:1] + s2]
    ci = [i1[0:1] * nk + i2]
    for a in range(1, SUBLANE):
        nb = k // (a + 1)
        cs.append(jnp.where(rid8 < nb, s1[a:a + 1] + s2[0:SUBLANE], -jnp.inf))
        ci.append(i1[a:a + 1] * nk + i2[0:SUBLANE])
    cs.append(s1[SUBLANE:k] + s2[0:1])
    ci.append(i1[SUBLANE:k] * nk + i2[0:1])
    _extract_topk(jnp.concatenate(cs, axis=0), k, ts_ref, te_ref, ids=jnp.concatenate(ci, axis=0))
    ts = ts_ref[...]
    ex = jnp.exp(ts - ts[0:1])
    g_ref[0] = ex / jnp.sum(ex, axis=0, keepdims=True)
    te = te_ref[...]
    e_ref[0] = (te >> 1) * SUBLANE
    par_ref[0] = (te & 1).astype(F32)


def _peer_topk(qp, subkeys_bf16, *, tt):
    n = qp.shape[0]
    h = PEER_HEADS
    k = PEER_TOPK
    nk = PEER_NKEYS
    ospec = pl.BlockSpec((1, k, tt), lambda i, hh: (hh, 0, i))
    return pl.pallas_call(
        _peer_topk_kernel,
        grid=(n // tt, h),
        in_specs=[pl.BlockSpec((tt, 2 * nk), lambda i, hh: (i, hh)),
                  pl.BlockSpec((1, 2, nk, nk), lambda i, hh: (hh, 0, 0, 0))],
        out_specs=[ospec, ospec, ospec],
        out_shape=[jax.ShapeDtypeStruct((h, k, n), I32), jax.ShapeDtypeStruct((h, k, n), F32),
                   jax.ShapeDtypeStruct((h, k, n), F32)],
        scratch_shapes=[pltpu.VMEM((k, tt), F32), pltpu.VMEM((k, tt), I32),
                        pltpu.VMEM((k, tt), F32), pltpu.VMEM((k, tt), I32),
                        pltpu.VMEM((k, tt), F32), pltpu.VMEM((k, tt), I32)],
        compiler_params=_cparams(("parallel", "arbitrary")),
        name="peer_topk",
    )(qp, subkeys_bf16)


def _row_as_tile(ref, t):
    row = ref[pl.ds(t, 1), :]
    return jnp.concatenate([row[:, k * LANE:(k + 1) * LANE] for k in range(SUBLANE)], axis=0)


def _lane_to_rows(row):
    return jnp.broadcast_to(row, (LANE, LANE)).T


def _sublane_sums8(ps, sub):
    lo4 = sub < 4
    c = []
    for i in range(4):
        a = jnp.where(lo4, ps[i], ps[i + 4])
        b = jnp.where(lo4, ps[i + 4], ps[i])
        c.append(a + pltpu.roll(b, 4, 0))
    m2 = (sub & 3) < 2
    dd = []
    for i in range(2):
        x = jnp.where(m2, c[i], pltpu.roll(c[i + 2], 2, 0))
        y = jnp.where(m2, pltpu.roll(c[i], 6, 0), c[i + 2])
        dd.append(x + y)
    m1 = (sub & 1) < 1
    x = jnp.where(m1, dd[0], pltpu.roll(dd[1], 1, 0))
    y = jnp.where(m1, pltpu.roll(dd[0], 7, 0), dd[1])
    return x + y


PEER_U_TOKENS_PER_ITER = 16


def _pack_expert_pairs(tab):
    ne, d = tab.shape
    bits = lax.bitcast_convert_type(tab.astype(BF16), jnp.uint16).astype(jnp.uint32).reshape(ne // 2, 2, d)
    word = (bits[:, 0] << 16) | bits[:, 1]
    return lax.bitcast_convert_type(word, I32).reshape(ne // 2 * SUBLANE, LANE)


def _expert_words(tab_ref, row0):
    return tab_ref[pl.ds(pl.multiple_of(row0, SUBLANE), SUBLANE), :]


def _peer_u_kernel(idx_ref, x_ref, g_ref, par_ref, tab_ref, alo_ref, ahi_ref, rall_ref, sh_ref, *, tb, npair):
    sub = lax.broadcasted_iota(I32, (SUBLANE, LANE), 0)
    upper = jnp.int32(-65536)
    nu = PEER_U_TOKENS_PER_ITER

    def one_token(t, u):
        x = _row_as_tile(x_ref, t)
        par = par_ref[pl.ds(t, 1), :]
        sh_ref[u] = (16.0 * _lane_to_rows(par)).astype(I32)
        for gi in range(npair // SUBLANE):
            ps = []
            for i in range(SUBLANE):
                j = gi * SUBLANE + i
                w = _expert_words(tab_ref, idx_ref[t, j])
                sh = jnp.broadcast_to(sh_ref[u, j:j + 1, :], (SUBLANE, LANE))
                ps.append(x * lax.bitcast_convert_type(lax.shift_left(w, sh) & upper, F32))
            rall_ref[u, gi * SUBLANE:(gi + 1) * SUBLANE, :] = _sublane_sums8(ps, sub)
        r = rall_ref[u]
        s = jnp.sum(r.T, axis=0, keepdims=True)
        a = jax.nn.gelu(s, approximate=True) * g_ref[pl.ds(t, 1), :]
        ahi = a * par
        ahi_ref[pl.ds(t, 1), :] = ahi
        alo_ref[pl.ds(t, 1), :] = a - ahi

    def tokens(i, carry):
        for u in range(nu):
            one_token(i * nu + u, u)
        return carry

    lax.fori_loop(0, tb // nu, tokens, 0)


def _peer_u(idx, x, g, par, tab, *, tb):
    n, npair = idx.shape
    assert npair == LANE and tb % PEER_U_TOKENS_PER_ITER == 0 and x.shape[1] == SUBLANE * LANE
    vrow = lambda: pl.BlockSpec((tb, npair), lambda i: (i, 0))
    return pl.pallas_call(
        functools.partial(_peer_u_kernel, tb=tb, npair=npair),
        grid=(n // tb,),
        in_specs=[pl.BlockSpec((tb, npair), lambda i: (i, 0), memory_space=pltpu.SMEM),
                  pl.BlockSpec((tb, SUBLANE * LANE), lambda i: (i, 0)),
                  vrow(), vrow(),
                  pl.BlockSpec(tab.shape, lambda i: (0, 0), pipeline_mode=pl.Buffered(1))],
        out_specs=[vrow(), vrow()],
        out_shape=[jax.ShapeDtypeStruct((n, npair), F32)] * 2,
        scratch_shapes=[pltpu.VMEM((PEER_U_TOKENS_PER_ITER, npair, LANE), F32),
                        pltpu.VMEM((PEER_U_TOKENS_PER_ITER, npair, LANE), I32)],
        compiler_params=_cparams(("parallel",)),
        name="peer_expert_in",
    )(idx, x, g, par, tab)


def _peer_v_consts(npair):
    rows = 2 * SUBLANE
    col = np.arange(npair * rows)
    expand = (col[None, :] // rows) == np.arange(npair)[:, None]
    s = np.arange(SUBLANE)[:, None]
    plo = (col[None, :] % rows) == 2 * s + 1
    phi = (col[None, :] % rows) == 2 * s
    return jnp.asarray(expand, BF16), jnp.asarray(plo, F32), jnp.asarray(phi, F32)


def _peer_v_kernel(idx_ref, alo_ref, ahi_ref, h_ref, gf_ref, ex_ref, plo_ref, phi_ref, tab_ref, o_ref,
                   elo_ref, ehi_ref, *, tb, npair):
    half = npair // 2
    kh = half * 2 * SUBLANE

    def lane_expand(a):
        ah = a.astype(BF16)
        al = (a - ah.astype(F32)).astype(BF16)
        return _dot(ah, ex_ref[...]) + _dot(al, ex_ref[...])

    elo_ref[...] = lane_expand(alo_ref[...])
    ehi_ref[...] = lane_expand(ahi_ref[...])

    def token(t, carry):
        tiles = [pltpu.bitcast(_expert_words(tab_ref, idx_ref[t, j]), BF16) for j in range(npair)]
        rhs = jnp.concatenate([jnp.concatenate(tiles[:half], axis=0),
                               jnp.concatenate(tiles[half:], axis=0)], axis=1)
        coef = elo_ref[pl.ds(t, 1), :] * plo_ref[...] + ehi_ref[pl.ds(t, 1), :] * phi_ref[...]
        c2 = jnp.concatenate([coef[:, :kh], coef[:, kh:]], axis=0)
        ch = c2.astype(BF16)
        cl = (c2 - ch.astype(F32)).astype(BF16)
        r = _dot(jnp.concatenate([ch, cl], axis=0), rhs)
        acc = ((r[0:SUBLANE, 0:LANE] + r[SUBLANE:2 * SUBLANE, LANE:])
               + (r[2 * SUBLANE:3 * SUBLANE, 0:LANE] + r[3 * SUBLANE:, LANE:]))
        hh = _row_as_tile(h_ref, t) + acc
        ss = jnp.sum(jnp.sum(hh * hh, axis=1, keepdims=True), axis=0, keepdims=True)
        res = hh * lax.rsqrt(ss * (1.0 / (SUBLANE * LANE)) + RMS_EPS) * gf_ref[...]
        o_ref[pl.ds(t, 1), :] = jnp.concatenate([res[k:k + 1, :] for k in range(SUBLANE)], axis=1)
        return carry

    lax.fori_loop(0, tb, token, 0, unroll=16)


def _peer_v(idx, alo, ahi, h, gfin, tab, *, tb):
    n, npair = idx.shape
    assert npair == LANE and h.shape[1] == SUBLANE * LANE
    ex, plo, phi = _peer_v_consts(npair)
    vrow = lambda: pl.BlockSpec((tb, npair), lambda i: (i, 0))
    full = lambda a: pl.BlockSpec(a.shape, lambda i: (0, 0))
    return pl.pallas_call(
        functools.partial(_peer_v_kernel, tb=tb, npair=npair),
        grid=(n // tb,),
        in_specs=[pl.BlockSpec((tb, npair), lambda i: (i, 0), memory_space=pltpu.SMEM),
                  vrow(), vrow(),
                  pl.BlockSpec((tb, SUBLANE * LANE), lambda i: (i, 0)),
                  pl.BlockSpec((SUBLANE, LANE), lambda i: (0, 0)),
                  full(ex), full(plo), full(phi),
                  pl.BlockSpec(tab.shape, lambda i: (0, 0), pipeline_mode=pl.Buffered(1))],
        out_specs=pl.BlockSpec((tb, SUBLANE * LANE), lambda i: (i, 0)),
        out_shape=jax.ShapeDtypeStruct(h.shape, F32),
        scratch_shapes=[pltpu.VMEM((tb, 2 * SUBLANE * npair), F32), pltpu.VMEM((tb, 2 * SUBLANE * npair), F32)],
        compiler_params=_cparams(("parallel",)),
        name="peer_expert_out",
    )(idx, alo, ahi, h, gfin, ex, plo, phi, tab)


def _pick(n, pref):
    t = min(n, pref)
    while n % t:
        t //= 2
    return t


def _layer(h, lb, p, fft_passes):
    bsz, seq, d = h.shape
    n = bsz * seq
    kw = HGRN_HEADS * HGRN_DK
    hyena_col0 = 3 * kw + 2 * d
    gate_col0 = hyena_col0 + 3 * d

    proj2 = _norm_matmul(h.reshape(n, d), p["norm_mix_g"], p["w_in"].astype(BF16),
                         tm=_pick(n, 1024), tn=1024)
    proj = proj2.reshape(bsz, seq, -1)

    oh = _hgrn(proj, lb, p["hgrn_norm_g"], tg=_pick(seq, 1024))

    u, x0c = _shortconv(proj, p["hyena_conv_w"], p["hyena_conv_b"], col0=hyena_col0, width=d,
                        tc=_pick(seq, 1024))
    kern = _hyena_conv_kernel(seq, p["filt_w1"], p["filt_b1"], p["filt_freq1"], p["filt_w2"], p["filt_b2"],
                              p["filt_freq2"], p["filt_w3"], p["filt_decay"], tl=_pick(seq, 512))
    n1 = 2 * seq // LANE
    consts = _fft_consts(n1)
    kf = _kernel_spectrum(kern.T.reshape(d, n1, LANE), consts, cb=_pick(d, 8), passes=3)
    u_t = jnp.swapaxes(u, 1, 2).reshape(bsz, d, n1 // 2, LANE)
    y_t = _fftconv(u_t, kf, consts, cb=_pick(d, 16), passes=fft_passes)
    yc = jnp.swapaxes(y_t.reshape(bsz, d, seq), 1, 2)

    h1 = _merge(oh.reshape(n, d), yc.reshape(n, d), u.reshape(n, d), x0c.reshape(n, d), proj2,
                h.reshape(n, d), p["hyena_bias"], p["w_branch_a"].astype(BF16),
                p["w_branch_b"].astype(BF16), p["w_out"].astype(BF16), gate_col0=gate_col0, tm=_pick(n, 256))

    qp, hn = _norm_matmul(h1, p["norm_ffn_g"], p["peer_w_q"].astype(BF16), tm=_pick(n, 1024),
                          tn=_pick(p["peer_w_q"].shape[1], 1024), emit_xn=True)
    tiles, gates, halves = _peer_topk(qp, p["peer_subkeys"].astype(BF16), tt=_pick(n, 256))
    npair = PEER_HEADS * PEER_TOPK
    tokmajor = lambda a: a.transpose(2, 0, 1).reshape(n, npair)
    idx = tokmajor(tiles)
    ne = p["peer_u"].shape[0]
    utab = _pack_expert_pairs(p["peer_u"])
    vtab = _pack_expert_pairs(p["peer_v"])
    tb = _pick(n, 128)
    alo, ahi = _peer_u(idx, hn, tokmajor(gates), tokmajor(halves), utab, tb=tb)
    return h1, idx, alo, ahi, vtab, tb


def kernel(x, norm_mix_g, w_in, hgrn_lb_logits, hgrn_norm_g, hyena_conv_w, hyena_conv_b, filt_w1, filt_b1, filt_freq1, filt_w2, filt_b2, filt_freq2, filt_w3, filt_decay, hyena_bias, w_branch_a, w_branch_b, w_out, norm_ffn_g, peer_w_q, peer_subkeys, peer_u, peer_v, norm_final_g):
    bsz, seq, d = x.shape
    n = bsz * seq
    depth = w_in.shape[0]
    assert depth == 1, "the fused final norm assumes a single layer"
    lb_table = jnp.cumsum(jax.nn.softmax(hgrn_lb_logits.astype(F32), axis=0), axis=0)
    p = dict(norm_mix_g=norm_mix_g[0], w_in=w_in[0], hgrn_norm_g=hgrn_norm_g[0], hyena_conv_w=hyena_conv_w[0],
             hyena_conv_b=hyena_conv_b[0], filt_w1=filt_w1[0], filt_b1=filt_b1[0], filt_freq1=filt_freq1[0],
             filt_w2=filt_w2[0], filt_b2=filt_b2[0], filt_freq2=filt_freq2[0], filt_w3=filt_w3[0],
             filt_decay=filt_decay[0], hyena_bias=hyena_bias[0], w_branch_a=w_branch_a[0],
             w_branch_b=w_branch_b[0], w_out=w_out[0], norm_ffn_g=norm_ffn_g[0], peer_w_q=peer_w_q[0],
             peer_subkeys=peer_subkeys[0], peer_u=peer_u[0], peer_v=peer_v[0])
    h1, idx, alo, ahi, vtab, tb = _layer(x, lb_table[0], p, fft_passes=1)
    out = _peer_v(idx, alo, ahi, h1, norm_final_g.reshape(SUBLANE, LANE), vtab, tb=tb)
    return out.reshape(bsz, seq, d)
```

```python
import functools
import math

import numpy as np
import jax
import jax.numpy as jnp
from jax import lax
from jax.experimental import pallas as pl
from jax.experimental.pallas import tpu as pltpu

F32 = jnp.float32
BF16 = jnp.bfloat16
I32 = jnp.int32

RMS_EPS = 1e-6
LANE = 128
SUBLANE = 8
VMEM_LIMIT = 56 * 1024 * 1024

HGRN_HEADS = 8
HGRN_DK = 128
HGRN_CHUNK = 128
HYENA_BANDS = 16
HYENA_HIDDEN = 64
PEER_HEADS = 8
PEER_NKEYS = 128
PEER_TOPK = 16

HIGHEST = lax.Precision.HIGHEST


def _cparams(sem, vmem=VMEM_LIMIT):
    return pltpu.CompilerParams(dimension_semantics=sem, vmem_limit_bytes=vmem)


def _dot(a, b):
    return jnp.dot(a, b, preferred_element_type=F32)


def _dot_nt(a, b):
    return lax.dot_general(a, b, (((1,), (1,)), ((), ())), preferred_element_type=F32)


def _normmm_kernel(x_ref, g_ref, w_ref, o_ref, *rest, emit_xn):
    if emit_xn:
        xn_out_ref, xn_ref = rest
    else:
        (xn_ref,) = rest

    @pl.when(pl.program_id(1) == 0)
    def _():
        x = x_ref[...]
        ms = jnp.mean(x * x, axis=-1, keepdims=True)
        xn = x * lax.rsqrt(ms + RMS_EPS) * g_ref[...]
        xn_ref[...] = xn.astype(BF16)
        if emit_xn:
            xn_out_ref[...] = xn

    o_ref[...] = _dot(xn_ref[...], w_ref[...])


def _norm_matmul(x, g, w_bf16, *, tm, tn, emit_xn=False):
    n, d = x.shape
    nout = w_bf16.shape[1]
    out_shape = [jax.ShapeDtypeStruct((n, nout), F32)]
    out_specs = [pl.BlockSpec((tm, tn), lambda i, j: (i, j))]
    if emit_xn:
        out_shape.append(jax.ShapeDtypeStruct((n, d), F32))
        out_specs.append(pl.BlockSpec((tm, d), lambda i, j: (i, 0)))
    res = pl.pallas_call(
        functools.partial(_normmm_kernel, emit_xn=emit_xn),
        grid=(n // tm, nout // tn),
        in_specs=[pl.BlockSpec((tm, d), lambda i, j: (i, 0)),
                  pl.BlockSpec((1, d), lambda i, j: (0, 0)),
                  pl.BlockSpec((d, tn), lambda i, j: (0, j))],
        out_specs=out_specs,
        out_shape=out_shape,
        scratch_shapes=[pltpu.VMEM((tm, d), BF16)],
        compiler_params=_cparams(("parallel", "arbitrary")),
        name="norm_matmul",
    )(x, g.reshape(1, d), w_bf16)
    return res if emit_xn else res[0]


def _hgrn_consts(c):
    nlev = int(math.log2(c))
    t = np.arange(c)
    m = np.zeros((2, (nlev + 2) * c, c), np.float32)
    up = np.zeros((2, nlev, c, LANE), np.float32)
    mask = np.zeros((nlev + 1, c, c), np.float32)
    for d in range(2):
        p = t if d == 0 else c - 1 - t
        m[d, 0:c] = p[None, :] <= p[:, None]
        for l in range(nlev):
            hs = 1 << l
            pmid = (p // (2 * hs)) * (2 * hs) + hs - 1
            m[d, (l + 1) * c:(l + 2) * c] = p[None, :] <= pmid[:, None]
            up[d, l] = (((p // hs) % 2) == 1)[:, None]
        m[d, (nlev + 1) * c:] = 1.0
    mask[0] = np.eye(c)
    for l in range(nlev):
        hs = 1 << l
        mask[l + 1] = (t[:, None] // (2 * hs)) == (t[None, :] // (2 * hs))
    m3 = np.concatenate([m, m, m], axis=2)
    return jnp.asarray(m3, BF16), jnp.asarray(up, F32), jnp.asarray(mask, F32), nlev


def _split3(x):
    hi = x.astype(BF16)
    r1 = x - hi.astype(F32)
    mid = r1.astype(BF16)
    lo = (r1 - mid.astype(F32)).astype(BF16)
    return hi, mid, lo


def _hgrn_kernel(q_ref, z_ref, v_ref, og_ref, lb_ref, gn_ref, m_ref, up_ref, mask_ref,
                 o_ref, st_ref, of_ref, *, c, nc, nlev, hpb):
    d = pl.program_id(2)
    g = pl.program_id(3)
    ng = pl.num_programs(3)
    tg = nc * c

    @pl.when(g == 0)
    def _():
        st_ref[...] = jnp.zeros_like(st_ref)

    grp = jnp.where(d == 0, g, ng - 1 - g)

    def head_chunk(hh, r0):
        sl = slice(hh * LANE, (hh + 1) * LANE)
        lb = lb_ref[:, sl]
        qraw = q_ref[0, pl.ds(r0, c), sl]
        z = z_ref[0, pl.ds(r0, c), sl]
        v = v_ref[0, pl.ds(r0, c), sl]
        logf = jnp.log(lb + (1.0 - lb) * jax.nn.sigmoid(z))
        kk = (1.0 - lb) * jax.nn.sigmoid(-z)
        q = qraw * jax.nn.sigmoid(qraw)

        l3 = jnp.concatenate(_split3(logf), axis=0)
        r = _dot(m_ref[0], l3)
        b = r[0:c]
        blast = r[(nlev + 1) * c:(nlev + 1) * c + 1]

        qb = q.astype(BF16)
        kb = kk.astype(BF16)
        sc = mask_ref[0] * _dot_nt(qb, kb)
        for l in range(nlev):
            rl = r[(l + 1) * c:(l + 2) * c]
            up = up_ref[0, l]
            e = jnp.exp((b - rl) * (2.0 * up - 1.0))
            eu = e * up
            qu = (q * eu).astype(BF16)
            kl = (kk * (e - eu)).astype(BF16)
            sc = sc + mask_ref[l + 1] * _dot_nt(qu, kl)

        vb = v.astype(BF16)
        st = st_ref[hh]
        inter = _dot_nt((q * jnp.exp(b)).astype(BF16), st.astype(BF16))
        o = inter + _dot(sc.astype(BF16), vb)
        kd = (kk * jnp.exp(jnp.minimum(blast - b, 0.0))).astype(BF16)
        st_ref[hh] = st * jnp.exp(blast) + _dot(v.T.astype(BF16), kd)
        return o

    def chunk(i, carry):
        ci = jnp.where(d == 0, i, nc - 1 - i)
        r0 = pl.multiple_of(ci * c, c)
        outs = [head_chunk(hh, r0) for hh in range(hpb)]
        row = pl.multiple_of(grp * tg + r0, c)

        @pl.when(d == 0)
        def _():
            for hh in range(hpb):
                of_ref[pl.ds(row, c), hh * LANE:(hh + 1) * LANE] = outs[hh]

        @pl.when(d == 1)
        def _():
            for hh in range(hpb):
                sl = slice(hh * LANE, (hh + 1) * LANE)
                tot = of_ref[pl.ds(row, c), sl] + outs[hh]
                ms = jnp.mean(tot * tot, axis=-1, keepdims=True)
                on = tot * lax.rsqrt(ms + RMS_EPS) * gn_ref[...]
                og = og_ref[0, pl.ds(r0, c), sl]
                o_ref[0, pl.ds(r0, c), sl] = (on * (og * jax.nn.sigmoid(og))).astype(BF16)

        return carry

    lax.fori_loop(0, nc, chunk, 0)


def _hgrn(proj, lb, gn, *, tg, hpb=4):
    bsz, seq, _ = proj.shape
    c = HGRN_CHUNK
    nc = tg // c
    ng = seq // tg
    h = HGRN_HEADS // hpb
    bw = hpb * LANE
    m, up, mask, nlev = _hgrn_consts(c)

    def rows(d, g):
        return jnp.where(d == 0, g, ng - 1 - g)

    def outrows(d, g):
        return jnp.where(d == 0, ng - 1, ng - 1 - g)

    in_specs = [
        pl.BlockSpec((1, tg, bw), lambda b, hh, d, g: (b, rows(d, g), hh)),
        pl.BlockSpec((1, tg, bw), lambda b, hh, d, g: (b, rows(d, g), h * (1 + d) + hh)),
        pl.BlockSpec((1, tg, bw), lambda b, hh, d, g: (b, rows(d, g), 3 * h + hh)),
        pl.BlockSpec((1, tg, bw), lambda b, hh, d, g: (b, outrows(d, g), 4 * h + hh)),
        pl.BlockSpec((1, bw), lambda b, hh, d, g: (0, hh)),
        pl.BlockSpec((1, LANE), lambda b, hh, d, g: (0, 0)),
        pl.BlockSpec((1,) + m.shape[1:], lambda b, hh, d, g: (d, 0, 0)),
        pl.BlockSpec((1,) + up.shape[1:], lambda b, hh, d, g: (d, 0, 0, 0)),
        pl.BlockSpec(mask.shape, lambda b, hh, d, g: (0, 0, 0)),
    ]
    return pl.pallas_call(
        functools.partial(_hgrn_kernel, c=c, nc=nc, nlev=nlev, hpb=hpb),
        grid=(bsz, h, 2, ng),
        in_specs=in_specs,
        out_specs=pl.BlockSpec((1, tg, bw), lambda b, hh, d, g: (b, outrows(d, g), hh)),
        out_shape=jax.ShapeDtypeStruct((bsz, seq, h * bw), BF16),
        scratch_shapes=[pltpu.VMEM((hpb, LANE, HGRN_DK), F32), pltpu.VMEM((seq, bw), F32)],
        compiler_params=_cparams(("parallel", "parallel", "arbitrary", "arbitrary")),
        name="hgrn2_scan",
    )(proj, proj, proj, proj, lb.reshape(1, -1), gn.reshape(1, -1), m, up, mask)


def _shortconv_kernel(*refs, tc):
    (x0_ref, x0p_ref, x0n_ref, x1_ref, x1p_ref, x1n_ref, x2_ref, x2p_ref, x2n_ref,
     w0_ref, w1_ref, w2_ref, b0_ref, b1_ref, b2_ref, u_ref, x0c_ref, ut_ref) = refs
    i = pl.program_id(2)
    n = pl.num_programs(2)
    has_prev = (i > 0).astype(F32)
    has_next = (i < n - 1).astype(F32)
    rid = lax.broadcasted_iota(I32, (tc, LANE), 0)

    def conv(x_ref, xp_ref, xn_ref, w_ref, b_ref):
        x = x_ref[0]
        prev_row = xp_ref[0, SUBLANE - 1:SUBLANE, :] * has_prev
        next_row = xn_ref[0, 0:1, :] * has_next
        xm1 = jnp.where(rid == 0, prev_row, pltpu.roll(x, 1, 0))
        xp1 = jnp.where(rid == tc - 1, next_row, pltpu.roll(x, tc - 1, 0))
        w = w_ref[...]
        return ((b_ref[...] + xm1 * w[0:1]) + x * w[1:2]) + xp1 * w[2:3]

    x0c = conv(x0_ref, x0p_ref, x0n_ref, w0_ref, b0_ref)
    x1c = conv(x1_ref, x1p_ref, x1n_ref, w1_ref, b1_ref)
    vc = conv(x2_ref, x2p_ref, x2n_ref, w2_ref, b2_ref)
    x0c_ref[0] = x0c
    u = vc * x1c
    u_ref[0] = u
    ut_ref[0] = u.T


def _shortconv(proj, conv_w, conv_b, *, col0, width, tc):
    bsz, seq, _ = proj.shape
    ncb = width // LANE
    cb0 = col0 // LANE
    nt = seq // tc
    rb = tc // SUBLANE
    nrb = seq // SUBLANE

    def stream(k):
        off = cb0 + k * ncb
        return [
            pl.BlockSpec((1, tc, LANE), lambda b, cc, i, off=off: (b, i, off + cc)),
            pl.BlockSpec((1, SUBLANE, LANE), lambda b, cc, i, off=off: (b, jnp.maximum(i * rb - 1, 0), off + cc)),
            pl.BlockSpec((1, SUBLANE, LANE), lambda b, cc, i, off=off: (b, jnp.minimum((i + 1) * rb, nrb - 1), off + cc)),
        ]

    wspecs = [pl.BlockSpec((3, LANE), lambda b, cc, i, k=k: (0, k * ncb + cc)) for k in range(3)]
    bspecs = [pl.BlockSpec((1, LANE), lambda b, cc, i, k=k: (0, k * ncb + cc)) for k in range(3)]
    ospec = pl.BlockSpec((1, tc, LANE), lambda b, cc, i: (b, i, cc))
    cb2 = conv_b.reshape(1, -1)
    return pl.pallas_call(
        functools.partial(_shortconv_kernel, tc=tc),
        grid=(bsz, ncb, nt),
        in_specs=stream(0) + stream(1) + stream(2) + wspecs + bspecs,
        out_specs=[ospec, ospec, pl.BlockSpec((1, LANE, tc), lambda b, cc, i: (b, cc, i))],
        out_shape=[jax.ShapeDtypeStruct((bsz, seq, width), F32)] * 2 + [jax.ShapeDtypeStruct((bsz, width, seq), F32)],
        compiler_params=_cparams(("parallel", "parallel", "arbitrary")),
        name="hyena_shortconv",
    )(*([proj] * 9), conv_w, conv_w, conv_w, cb2, cb2, cb2)


def _filter_kernel(band_ref, w1_ref, b1_ref, f1_ref, w2_ref, b2_ref, f2_ref, w3_ref, dec_ref, o_ref,
                   *, tl, seq):
    i = pl.program_id(0)
    n = i * tl + lax.broadcasted_iota(I32, (tl, LANE), 0)
    pos = jnp.where(n < seq, n, 2 * seq - n).astype(F32)
    lane = lax.broadcasted_iota(I32, (tl, LANE), 1)
    t = pos / float(max(seq - 1, 1))
    ang = (2.0 * math.pi / seq) * pos * band_ref[...]
    z = jnp.where(lane == 0, t,
                  jnp.where(lane <= HYENA_BANDS, jnp.cos(ang),
                            jnp.where(lane <= 2 * HYENA_BANDS, -jnp.sin(ang), 0.0)))
    hid = jnp.sin(f1_ref[...] * (jnp.dot(z, w1_ref[...], precision=HIGHEST, preferred_element_type=F32) + b1_ref[...]))
    hid = jnp.sin(f2_ref[...] * (jnp.dot(hid, w2_ref[...], precision=HIGHEST, preferred_element_type=F32) + b2_ref[...]))
    filt = jnp.dot(hid, w3_ref[...], precision=HIGHEST, preferred_element_type=F32)
    filt = filt * jnp.exp(-t[:, 0:1] * jnp.abs(dec_ref[...]))
    o_ref[...] = jnp.where(n[:, 0:1] == seq, 0.0, filt)


def _hyena_conv_kernel(seq, w1, b1, f1, w2, b2, f2, w3, decay, *, tl):
    emb, hid = w1.shape
    wout = w3.shape[1] // 2
    nb = seq // tl
    bands = jnp.linspace(1e-4, HYENA_BANDS - 1, HYENA_BANDS, dtype=F32)
    band_row = jnp.zeros((1, LANE), F32).at[0, 1:1 + HYENA_BANDS].set(bands)
    band_row = band_row.at[0, 1 + HYENA_BANDS:1 + 2 * HYENA_BANDS].set(bands)

    def padm(a, r, c):
        return jnp.zeros((r, c), F32).at[:a.shape[0], :a.shape[1]].set(a.astype(F32))

    w1p = padm(w1, LANE, LANE)
    w2p = padm(w2, LANE, LANE)
    w3p = padm(w3, LANE, 2 * wout)
    vec = lambda a: padm(a.reshape(1, -1), 1, LANE)
    full = lambda shp: pl.BlockSpec(shp, lambda i: (0, 0))
    half = lambda rows: pl.BlockSpec((rows, wout), lambda i: (0, i // nb))
    return pl.pallas_call(
        functools.partial(_filter_kernel, tl=tl, seq=seq),
        grid=(2 * nb,),
        in_specs=[full((1, LANE)), full((LANE, LANE)), full((1, LANE)), full((1, LANE)),
                  full((LANE, LANE)), full((1, LANE)), full((1, LANE)), half(LANE), half(1)],
        out_specs=pl.BlockSpec((tl, wout), lambda i: (i, 0)),
        out_shape=jax.ShapeDtypeStruct((2 * seq, wout), F32),
        compiler_params=_cparams(("parallel",)),
        name="hyena_filter_mlp",
    )(band_row, w1p, vec(b1), vec(f1), w2p, vec(b2), vec(f2), w3p, decay.reshape(1, -1).astype(F32))


def _fft_consts(n1):
    n2 = LANE
    n = n1 * n2
    k1 = np.arange(n1)
    f1 = np.exp(-2j * np.pi * np.outer(k1, k1) / n1)
    k2 = np.arange(n2)
    f2 = np.exp(-2j * np.pi * np.outer(k2, k2) / n2)
    tw = np.exp(-2j * np.pi * np.outer(k1, k2) / n)
    f1s = np.concatenate([f1.real, f1.imag], axis=0)
    g2 = np.block([[f2.real, f2.imag], [-f2.imag, f2.real]])
    g2c = np.block([[f2.real, -f2.imag], [f2.imag, f2.real]])
    twc = np.concatenate([tw.real, tw.imag], axis=1)

    def hilo(a):
        a32 = jnp.asarray(a, F32)
        hi = a32.astype(BF16)
        lo = (a32 - hi.astype(F32)).astype(BF16)
        return jnp.stack([hi, lo])

    return dict(f1s=hilo(f1s), g2=hilo(g2), g2c=hilo(g2c), tw=jnp.asarray(twc, F32))


def _mm_const_lhs(c_ref, x, passes):
    xh = x.astype(BF16)
    out = _dot(c_ref[0], xh)
    if passes >= 2:
        out = out + _dot(c_ref[0], (x - xh.astype(F32)).astype(BF16))
    if passes >= 3:
        out = out + _dot(c_ref[1], xh)
    return out


def _mm_const_rhs(x, c_ref, passes):
    xh = x.astype(BF16)
    out = _dot(xh, c_ref[0])
    if passes >= 2:
        out = out + _dot((x - xh.astype(F32)).astype(BF16), c_ref[0])
    if passes >= 3:
        out = out + _dot(xh, c_ref[1])
    return out


def _kfft_kernel(x_ref, f1s_ref, tw_ref, g2_ref, o_ref, *, n1, cb, scale, passes):
    tr = tw_ref[:, 0:LANE]
    ti = tw_ref[:, LANE:2 * LANE]

    def body(cidx, carry):
        x = x_ref[cidx]
        p = _mm_const_lhs(f1s_ref, x, passes)
        ar = p[0:n1]
        ai = p[n1:2 * n1]
        br = ar * tr - ai * ti
        bi = ar * ti + ai * tr
        o_ref[cidx] = _mm_const_rhs(jnp.concatenate([br, bi], axis=1), g2_ref, passes) * scale
        return carry

    lax.fori_loop(0, cb, body, 0, unroll=2)


def _kernel_spectrum(kern_t, consts, *, cb, passes):
    w, n1, _ = kern_t.shape
    full3 = lambda a: pl.BlockSpec(a.shape, lambda i: (0, 0, 0))
    return pl.pallas_call(
        functools.partial(_kfft_kernel, n1=n1, cb=cb, scale=1.0 / (n1 * LANE), passes=passes),
        grid=(w // cb,),
        in_specs=[pl.BlockSpec((cb, n1, LANE), lambda i: (i, 0, 0)),
                  full3(consts["f1s"]),
                  pl.BlockSpec(consts["tw"].shape, lambda i: (0, 0)),
                  full3(consts["g2"])],
        out_specs=pl.BlockSpec((cb, n1, 2 * LANE), lambda i: (i, 0, 0)),
        out_shape=jax.ShapeDtypeStruct((w, n1, 2 * LANE), F32),
        compiler_params=_cparams(("parallel",)),
        name="hyena_filter_fft",
    )(kern_t, consts["f1s"], consts["tw"], consts["g2"])


def _fftconv_kernel(u_ref, kf_ref, f1a_ref, f1b_ref, tw_ref, g2_ref, g2c_ref, o_ref, *, n1, cb, passes):
    n1h = n1 // 2
    tr = tw_ref[:, 0:LANE]
    ti = tw_ref[:, LANE:2 * LANE]

    def body(cidx, carry):
        z = jnp.concatenate([u_ref[0, cidx], u_ref[1, cidx]], axis=1)
        p = _mm_const_lhs(f1a_ref, z, passes)
        ar = p[0:n1, 0:LANE] - p[n1:, LANE:]
        ai = p[0:n1, LANE:] + p[n1:, 0:LANE]
        a2 = jnp.concatenate([ar * tr - ai * ti, ar * ti + ai * tr], axis=1)
        x = _mm_const_rhs(a2, g2_ref, passes)
        xr = x[:, 0:LANE]
        xi = x[:, LANE:]
        kf = kf_ref[cidx]
        kr = kf[:, 0:LANE]
        ki = kf[:, LANE:]
        y = jnp.concatenate([xr * kr - xi * ki, xr * ki + xi * kr], axis=1)
        bm = _mm_const_rhs(y, g2c_ref, passes)
        br = bm[:, 0:LANE]
        bi = bm[:, LANE:]
        b2 = jnp.concatenate([br * tr + bi * ti, bi * tr - br * ti], axis=1)
        p2 = _mm_const_lhs(f1b_ref, b2, passes)
        o_ref[0, cidx] = p2[0:n1h, 0:LANE] + p2[n1h:, LANE:]
        o_ref[1, cidx] = p2[0:n1h, LANE:] - p2[n1h:, 0:LANE]
        return carry

    lax.fori_loop(0, cb, body, 0, unroll=4)


def _fftconv(u_t, kf, consts, *, cb, passes):
    bsz, w, n1h, _ = u_t.shape
    n1 = 2 * n1h
    f1s = consts["f1s"]
    f1a = f1s[:, :, :n1h]
    f1b = jnp.concatenate([f1s[:, 0:n1h, :], f1s[:, n1:n1 + n1h, :]], axis=1)
    full3 = lambda a: pl.BlockSpec(a.shape, lambda i, j: (0, 0, 0))
    return pl.pallas_call(
        functools.partial(_fftconv_kernel, n1=n1, cb=cb, passes=passes),
        grid=(bsz // 2, w // cb),
        in_specs=[pl.BlockSpec((2, cb, n1h, LANE), lambda i, j: (i, j, 0, 0)),
                  pl.BlockSpec((cb, n1, 2 * LANE), lambda i, j: (j, 0, 0)),
                  full3(f1a), full3(f1b),
                  pl.BlockSpec(consts["tw"].shape, lambda i, j: (0, 0)),
                  full3(consts["g2"]), full3(consts["g2c"])],
        out_specs=pl.BlockSpec((2, cb, n1h, LANE), lambda i, j: (i, j, 0, 0)),
        out_shape=jax.ShapeDtypeStruct(u_t.shape, F32),
        compiler_params=_cparams(("parallel", "parallel")),
        name="hyena_fftconv",
    )(u_t, kf, f1a, f1b, consts["tw"], consts["g2"], consts["g2c"])


def _merge_kernel(oh_ref, yc_ref, u_ref, x0_ref, ga_ref, gb_ref, x_ref, hb_ref, wa_ref, wb_ref, wo_ref, o_ref):
    ya = _dot(oh_ref[...], wa_ref[...])
    ybp = x0_ref[...] * (yc_ref[0].T + u_ref[...] * hb_ref[...])
    yb = _dot(ybp.astype(BF16), wb_ref[...])
    merged = jax.nn.sigmoid(ga_ref[...]) * ya + jax.nn.sigmoid(gb_ref[...]) * yb
    o_ref[...] = x_ref[...] + _dot(merged.astype(BF16), wo_ref[...])


def _merge(oh, yc_cm, u, x0c, proj2, x2, hbias, wa, wb, wo, *, gate_col0, tm):
    n, d = x2.shape
    gc = gate_col0 // d
    nrb = yc_cm.shape[2] // tm
    row = lambda c=0: pl.BlockSpec((tm, d), lambda i, c=c: (i, c))
    wfull = pl.BlockSpec((d, d), lambda i: (0, 0))
    return pl.pallas_call(
        _merge_kernel,
        grid=(n // tm,),
        in_specs=[row(), pl.BlockSpec((1, d, tm), lambda i: (i // nrb, 0, i % nrb)), row(), row(), row(gc), row(gc + 1), row(),
                  pl.BlockSpec((1, d), lambda i: (0, 0)), wfull, wfull, wfull],
        out_specs=row(),
        out_shape=jax.ShapeDtypeStruct((n, d), F32),
        compiler_params=_cparams(("parallel",)),
        name="branch_merge",
    )(oh, yc_cm, u, x0c, proj2, proj2, x2, hbias.reshape(1, d), wa, wb, wo)


def _extract_topk(s, k, vals_ref, idx_ref, ids=None):
    nrow = s.shape[0]
    rid = lax.broadcasted_iota(I32, s.shape, 0)
    for j in range(k):
        m = jnp.max(s, axis=0, keepdims=True)
        ix = jnp.min(jnp.where(s == m, rid, nrow), axis=0, keepdims=True)
        hit = rid == ix
        vals_ref[j:j + 1, :] = m
        if ids is None:
            idx_ref[j:j + 1, :] = ix
        else:
            idx_ref[j:j + 1, :] = jnp.max(jnp.where(hit, ids, -1), axis=0, keepdims=True)
        s = jnp.where(hit, -jnp.inf, s)


def _peer_topk_kernel(q_ref, sk_ref, e_ref, g_ref, par_ref, s1_ref, i1_ref, s2_ref, i2_ref, ts_ref, te_ref):
    k = PEER_TOPK
    nk = PEER_NKEYS
    q = q_ref[...]
    s_a = _dot_nt(sk_ref[0, 0], q[:, 0:nk].astype(BF16))
    s_b = _dot_nt(sk_ref[0, 1], q[:, nk:2 * nk].astype(BF16))
    _extract_topk(s_a, k, s1_ref, i1_ref)
    _extract_topk(s_b, k, s2_ref, i2_ref)
    s1 = s1_ref[...]
    i1 = i1_ref[...]
    s2 = s2_ref[...]
    i2 = i2_ref[...]
    rid8 = lax.broadcasted_iota(I32, (SUBLANE, s1.shape[1]), 0)
    cs = [s1[0:1] + s2]
    ci = [i1[0:1] * nk + i2]
    for a in range(1, SUBLANE):
        nb = k // (a + 1)
        cs.append(jnp.where(rid8 < nb, s1[a:a + 1] + s2[0:SUBLANE], -jnp.inf))
        ci.append(i1[a:a + 1] * nk + i2[0:SUBLANE])
    cs.append(s1[SUBLANE:k] + s2[0:1])
    ci.append(i1[SUBLANE:k] * nk + i2[0:1])
    _extract_topk(jnp.concatenate(cs, axis=0), k, ts_ref, te_ref, ids=jnp.concatenate(ci, axis=0))
    ts = ts_ref[...]
    ex = jnp.exp(ts - ts[0:1])
    g_ref[0] = ex / jnp.sum(ex, axis=0, keepdims=True)
    te = te_ref[...]
    e_ref[0] = (te >> 1) * SUBLANE
    par_ref[0] = (te & 1).astype(F32)


def _peer_topk(qp, subkeys_bf16, *, tt):
    n = qp.shape[0]
    h = PEER_HEADS
    k = PEER_TOPK
    nk = PEER_NKEYS
    ospec = pl.BlockSpec((1, k, tt), lambda i, hh: (hh, 0, i))
    return pl.pallas_call(
        _peer_topk_kernel,
        grid=(n // tt, h),
        in_specs=[pl.BlockSpec((tt, 2 * nk), lambda i, hh: (i, hh)),
                  pl.BlockSpec((1, 2, nk, nk), lambda i, hh: (hh, 0, 0, 0))],
        out_specs=[ospec, ospec, ospec],
        out_shape=[jax.ShapeDtypeStruct((h, k, n), I32), jax.ShapeDtypeStruct((h, k, n), F32),
                   jax.ShapeDtypeStruct((h, k, n), F32)],
        scratch_shapes=[pltpu.VMEM((k, tt), F32), pltpu.VMEM((k, tt), I32),
                        pltpu.VMEM((k, tt), F32), pltpu.VMEM((k, tt), I32),
                        pltpu.VMEM((k, tt), F32), pltpu.VMEM((k, tt), I32)],
        compiler_params=_cparams(("parallel", "arbitrary")),
        name="peer_topk",
    )(qp, subkeys_bf16)


def _row_as_tile(ref, t):
    row = ref[pl.ds(t, 1), :]
    return jnp.concatenate([row[:, k * LANE:(k + 1) * LANE] for k in range(SUBLANE)], axis=0)


def _lane_to_rows(row):
    return jnp.broadcast_to(row, (LANE, LANE)).T


def _sublane_sums8(ps, sub):
    lo4 = sub < 4
    c = []
    for i in range(4):
        a = jnp.where(lo4, ps[i], ps[i + 4])
        b = jnp.where(lo4, ps[i + 4], ps[i])
        c.append(a + pltpu.roll(b, 4, 0))
    m2 = (sub & 3) < 2
    dd = []
    for i in range(2):
        x = jnp.where(m2, c[i], pltpu.roll(c[i + 2], 2, 0))
        y = jnp.where(m2, pltpu.roll(c[i], 6, 0), c[i + 2])
        dd.append(x + y)
    m1 = (sub & 1) < 1
    x = jnp.where(m1, dd[0], pltpu.roll(dd[1], 1, 0))
    y = jnp.where(m1, pltpu.roll(dd[0], 7, 0), dd[1])
    return x + y


PEER_U_TOKENS_PER_ITER = 16


def _pack_expert_pairs(tab):
    ne, d = tab.shape
    bits = lax.bitcast_convert_type(tab.astype(BF16), jnp.uint16).astype(jnp.uint32).reshape(ne // 2, 2, d)
    word = (bits[:, 0] << 16) | bits[:, 1]
    return lax.bitcast_convert_type(word, I32).reshape(ne // 2 * SUBLANE, LANE)


def _expert_words(tab_ref, row0):
    return tab_ref[pl.ds(pl.multiple_of(row0, SUBLANE), SUBLANE), :]


def _peer_u_kernel(idx_ref, x_ref, g_ref, par_ref, tab_ref, alo_ref, ahi_ref, rall_ref, sh_ref, *, tb, npair):
    sub = lax.broadcasted_iota(I32, (SUBLANE, LANE), 0)
    upper = jnp.int32(-65536)
    nu = PEER_U_TOKENS_PER_ITER

    def one_token(t, u):
        x = _row_as_tile(x_ref, t)
        par = par_ref[pl.ds(t, 1), :]
        sh_ref[u] = (16.0 * _lane_to_rows(par)).astype(I32)
        for gi in range(npair // SUBLANE):
            ps = []
            for i in range(SUBLANE):
                j = gi * SUBLANE + i
                w = _expert_words(tab_ref, idx_ref[t, j])
                sh = jnp.broadcast_to(sh_ref[u, j:j + 1, :], (SUBLANE, LANE))
                ps.append(x * lax.bitcast_convert_type(lax.shift_left(w, sh) & upper, F32))
            rall_ref[u, gi * SUBLANE:(gi + 1) * SUBLANE, :] = _sublane_sums8(ps, sub)
        r = rall_ref[u]
        s = jnp.sum(r.T, axis=0, keepdims=True)
        a = jax.nn.gelu(s, approximate=True) * g_ref[pl.ds(t, 1), :]
        ahi = a * par
        ahi_ref[pl.ds(t, 1), :] = ahi
        alo_ref[pl.ds(t, 1), :] = a - ahi

    def tokens(i, carry):
        for u in range(nu):
            one_token(i * nu + u, u)
        return carry

    lax.fori_loop(0, tb // nu, tokens, 0)


def _peer_u(idx, x, g, par, tab, *, tb):
    n, npair = idx.shape
    assert npair == LANE and tb % PEER_U_TOKENS_PER_ITER == 0 and x.shape[1] == SUBLANE * LANE
    vrow = lambda: pl.BlockSpec((tb, npair), lambda i: (i, 0))
    return pl.pallas_call(
        functools.partial(_peer_u_kernel, tb=tb, npair=npair),
        grid=(n // tb,),
        in_specs=[pl.BlockSpec((tb, npair), lambda i: (i, 0), memory_space=pltpu.SMEM),
                  pl.BlockSpec((tb, SUBLANE * LANE), lambda i: (i, 0)),
                  vrow(), vrow(),
                  pl.BlockSpec(tab.shape, lambda i: (0, 0), pipeline_mode=pl.Buffered(1))],
        out_specs=[vrow(), vrow()],
        out_shape=[jax.ShapeDtypeStruct((n, npair), F32)] * 2,
        scratch_shapes=[pltpu.VMEM((PEER_U_TOKENS_PER_ITER, npair, LANE), F32),
                        pltpu.VMEM((PEER_U_TOKENS_PER_ITER, npair, LANE), I32)],
        compiler_params=_cparams(("parallel",)),
        name="peer_expert_in",
    )(idx, x, g, par, tab)


def _peer_v_consts(npair):
    rows = 2 * SUBLANE
    col = np.arange(npair * rows)
    expand = (col[None, :] // rows) == np.arange(npair)[:, None]
    s = np.arange(SUBLANE)[:, None]
    plo = (col[None, :] % rows) == 2 * s + 1
    phi = (col[None, :] % rows) == 2 * s
    return jnp.asarray(expand, BF16), jnp.asarray(plo, F32), jnp.asarray(phi, F32)


def _peer_v_kernel(idx_ref, alo_ref, ahi_ref, h_ref, gf_ref, ex_ref, plo_ref, phi_ref, tab_ref, o_ref,
                   elo_ref, ehi_ref, *, tb, npair):
    half = npair // 2
    kh = half * 2 * SUBLANE

    def lane_expand(a):
        ah = a.astype(BF16)
        al = (a - ah.astype(F32)).astype(BF16)
        return _dot(ah, ex_ref[...]) + _dot(al, ex_ref[...])

    elo_ref[...] = lane_expand(alo_ref[...])
    ehi_ref[...] = lane_expand(ahi_ref[...])

    def token(t, carry):
        tiles = [pltpu.bitcast(_expert_words(tab_ref, idx_ref[t, j]), BF16) for j in range(npair)]
        rhs = jnp.concatenate([jnp.concatenate(tiles[:half], axis=0),
                               jnp.concatenate(tiles[half:], axis=0)], axis=1)
        coef = elo_ref[pl.ds(t, 1), :] * plo_ref[...] + ehi_ref[pl.ds(t, 1), :] * phi_ref[...]
        c2 = jnp.concatenate([coef[:, :kh], coef[:, kh:]], axis=0)
        ch = c2.astype(BF16)
        cl = (c2 - ch.astype(F32)).astype(BF16)
        r = _dot(jnp.concatenate([ch, cl], axis=0), rhs)
        acc = ((r[0:SUBLANE, 0:LANE] + r[SUBLANE:2 * SUBLANE, LANE:])
               + (r[2 * SUBLANE:3 * SUBLANE, 0:LANE] + r[3 * SUBLANE:, LANE:]))
        hh = _row_as_tile(h_ref, t) + acc
        ss = jnp.sum(jnp.sum(hh * hh, axis=1, keepdims=True), axis=0, keepdims=True)
        res = hh * lax.rsqrt(ss * (1.0 / (SUBLANE * LANE)) + RMS_EPS) * gf_ref[...]
        o_ref[pl.ds(t, 1), :] = jnp.concatenate([res[k:k + 1, :] for k in range(SUBLANE)], axis=1)
        return carry

    lax.fori_loop(0, tb, token, 0, unroll=16)


def _peer_v(idx, alo, ahi, h, gfin, tab, *, tb):
    n, npair = idx.shape
    assert npair == LANE and h.shape[1] == SUBLANE * LANE
    ex, plo, phi = _peer_v_consts(npair)
    vrow = lambda: pl.BlockSpec((tb, npair), lambda i: (i, 0))
    full = lambda a: pl.BlockSpec(a.shape, lambda i: (0, 0))
    return pl.pallas_call(
        functools.partial(_peer_v_kernel, tb=tb, npair=npair),
        grid=(n // tb,),
        in_specs=[pl.BlockSpec((tb, npair), lambda i: (i, 0), memory_space=pltpu.SMEM),
                  vrow(), vrow(),
                  pl.BlockSpec((tb, SUBLANE * LANE), lambda i: (i, 0)),
                  pl.BlockSpec((SUBLANE, LANE), lambda i: (0, 0)),
                  full(ex), full(plo), full(phi),
                  pl.BlockSpec(tab.shape, lambda i: (0, 0), pipeline_mode=pl.Buffered(1))],
        out_specs=pl.BlockSpec((tb, SUBLANE * LANE), lambda i: (i, 0)),
        out_shape=jax.ShapeDtypeStruct(h.shape, F32),
        scratch_shapes=[pltpu.VMEM((tb, 2 * SUBLANE * npair), F32), pltpu.VMEM((tb, 2 * SUBLANE * npair), F32)],
        compiler_params=_cparams(("parallel",)),
        name="peer_expert_out",
    )(idx, alo, ahi, h, gfin, ex, plo, phi, tab)


def _pick(n, pref):
    t = min(n, pref)
    while n % t:
        t //= 2
    return t


def _layer(h, lb, p, fft_passes):
    bsz, seq, d = h.shape
    n = bsz * seq
    kw = HGRN_HEADS * HGRN_DK
    hyena_col0 = 3 * kw + 2 * d
    gate_col0 = hyena_col0 + 3 * d

    proj2 = _norm_matmul(h.reshape(n, d), p["norm_mix_g"], p["w_in"].astype(BF16),
                         tm=_pick(n, 1024), tn=1024)
    proj = proj2.reshape(bsz, seq, -1)

    oh = _hgrn(proj, lb, p["hgrn_norm_g"], tg=_pick(seq, 1024))

    u, x0c, u_cm = _shortconv(proj, p["hyena_conv_w"], p["hyena_conv_b"], col0=hyena_col0, width=d,
                        tc=_pick(seq, 1024))
    kern = _hyena_conv_kernel(seq, p["filt_w1"], p["filt_b1"], p["filt_freq1"], p["filt_w2"], p["filt_b2"],
                              p["filt_freq2"], p["filt_w3"], p["filt_decay"], tl=_pick(seq, 512))
    n1 = 2 * seq // LANE
    consts = _fft_consts(n1)
    kf = _kernel_spectrum(kern.T.reshape(d, n1, LANE), consts, cb=_pick(d, 8), passes=3)
    u_t = u_cm.reshape(bsz, d, n1 // 2, LANE)
    y_t = _fftconv(u_t, kf, consts, cb=_pick(d, 16), passes=fft_passes)
    h1 = _merge(oh.reshape(n, d), y_t.reshape(bsz, d, seq), u.reshape(n, d), x0c.reshape(n, d), proj2,
                h.reshape(n, d), p["hyena_bias"], p["w_branch_a"].astype(BF16),
                p["w_branch_b"].astype(BF16), p["w_out"].astype(BF16), gate_col0=gate_col0, tm=_pick(n, 256))

    qp, hn = _norm_matmul(h1, p["norm_ffn_g"], p["peer_w_q"].astype(BF16), tm=_pick(n, 1024),
                          tn=_pick(p["peer_w_q"].shape[1], 1024), emit_xn=True)
    tiles, gates, halves = _peer_topk(qp, p["peer_subkeys"].astype(BF16), tt=_pick(n, 256))
    npair = PEER_HEADS * PEER_TOPK
    tokmajor = lambda a: a.transpose(2, 0, 1).reshape(n, npair)
    idx = tokmajor(tiles)
    ne = p["peer_u"].shape[0]
    utab = _pack_expert_pairs(p["peer_u"])
    vtab = _pack_expert_pairs(p["peer_v"])
    tb = _pick(n, 128)
    alo, ahi = _peer_u(idx, hn, tokmajor(gates), tokmajor(halves), utab, tb=tb)
    return h1, idx, alo, ahi, vtab, tb


def kernel(x, norm_mix_g, w_in, hgrn_lb_logits, hgrn_norm_g, hyena_conv_w, hyena_conv_b, filt_w1, filt_b1, filt_freq1, filt_w2, filt_b2, filt_freq2, filt_w3, filt_decay, hyena_bias, w_branch_a, w_branch_b, w_out, norm_ffn_g, peer_w_q, peer_subkeys, peer_u, peer_v, norm_final_g):
    bsz, seq, d = x.shape
    n = bsz * seq
    depth = w_in.shape[0]
    assert depth == 1, "the fused final norm assumes a single layer"
    lb_table = jnp.cumsum(jax.nn.softmax(hgrn_lb_logits.astype(F32), axis=0), axis=0)
    p = dict(norm_mix_g=norm_mix_g[0], w_in=w_in[0], hgrn_norm_g=hgrn_norm_g[0], hyena_conv_w=hyena_conv_w[0],
             hyena_conv_b=hyena_conv_b[0], filt_w1=filt_w1[0], filt_b1=filt_b1[0], filt_freq1=filt_freq1[0],
             filt_w2=filt_w2[0], filt_b2=filt_b2[0], filt_freq2=filt_freq2[0], filt_w3=filt_w3[0],
             filt_decay=filt_decay[0], hyena_bias=hyena_bias[0], w_branch_a=w_branch_a[0],
             w_branch_b=w_branch_b[0], w_out=w_out[0], norm_ffn_g=norm_ffn_g[0], peer_w_q=peer_w_q[0],
             peer_subkeys=peer_subkeys[0], peer_u=peer_u[0], peer_v=peer_v[0])
    h1, idx, alo, ahi, vtab, tb = _layer(x, lb_table[0], p, fft_passes=1)
    out = _peer_v(idx, alo, ahi, h1, norm_final_g.reshape(SUBLANE, LANE), vtab, tb=tb)
    return out.reshape(bsz, seq, d)
```
